```python
import math
import jax, jax.numpy as jnp
from jax import lax
import numpy as np

D_MODEL = 1024
BATCH = 8
SEQ = 2048
DEPTH = 1
DEC_BATCH = 128
DEC_SEQ = 8
PAST_LEN = 16384
PAGE_SIZE = 128

S5_WIDTH = D_MODEL // 2
S5_GROUP = 16
S5_GROUPS = S5_WIDTH // S5_GROUP
S5_STATE = 64
LRU_WIDTH = D_MODEL
LRU_HEADS = 8
LRU_HEAD_DIM = LRU_WIDTH // LRU_HEADS
CONV_WIDTH = 4
LRU_C = 8.0
IN_COLS = S5_WIDTH + 2 * LRU_WIDTH + 2 * D_MODEL
SPLITS = (S5_WIDTH, S5_WIDTH + LRU_WIDTH, S5_WIDTH + 2 * LRU_WIDTH, S5_WIDTH + 2 * LRU_WIDTH + D_MODEL)
PEER_HEADS = 8
N_KEYS = 128
N_EXPERTS = N_KEYS * N_KEYS
PEER_TOPK = 16
KEY_HALF = 128
QUERY_DIM = 2 * KEY_HALF
PEER_BLOCK = 128
PLE_DIM = 256
EPS = 1e-6

kernel_name = "hybrid_s5_rglru_peer_decode_step"

F32 = jnp.float32


def rms_norm(x, g):
    xf = x.astype(F32)
    y = xf * lax.rsqrt(jnp.mean(xf * xf, axis=-1, keepdims=True) + EPS)
    return (y * g.astype(F32)).astype(x.dtype)


def _cplx_combine(e1, e2):
    a1r, a1i, b1r, b1i = e1
    a2r, a2i, b2r, b2i = e2
    ar = a2r * a1r - a2i * a1i
    ai = a2r * a1i + a2i * a1r
    br = a2r * b1r - a2i * b1i + b2r
    bi = a2r * b1i + a2i * b1r + b2i
    return ar, ai, br, bi


def _lin_combine(e1, e2):
    a1, b1 = e1
    a2, b2 = e2
    return a1 * a2, a2 * b1 + b2


def s5_branch(u, h0_re, h0_im, lam_re, lam_im, log_dt, b_re, b_im, c_re, c_im, d_skip, w_glu, b_glu):
    n, l, _ = u.shape
    uf = u.astype(F32).reshape(n, l, S5_GROUPS, S5_GROUP)
    dt = jnp.exp(log_dt.astype(F32))[:, None]
    lr, li = lam_re.astype(F32), lam_im.astype(F32)
    mag = jnp.exp(lr * dt)
    abr, abi = mag * jnp.cos(li * dt), mag * jnp.sin(li * dt)
    den = lr * lr + li * li
    nr, ni = abr - 1.0, abi
    coef_r = (nr * lr + ni * li) / den
    coef_i = (ni * lr - nr * li) / den
    br_, bi_ = b_re.astype(F32), b_im.astype(F32)
    bbr = coef_r[..., None] * br_ - coef_i[..., None] * bi_
    bbi = coef_r[..., None] * bi_ + coef_i[..., None] * br_
    bur = jnp.einsum('nlgc,gpc->nlgp', uf, bbr)
    bui = jnp.einsum('nlgc,gpc->nlgp', uf, bbi)
    h0r, h0i = h0_re.astype(F32), h0_im.astype(F32)
    bur = bur.at[:, 0].add(abr * h0r - abi * h0i)
    bui = bui.at[:, 0].add(abr * h0i + abi * h0r)
    ar = jnp.broadcast_to(abr, bur.shape)
    ai = jnp.broadcast_to(abi, bui.shape)
    _, _, hr, hi = lax.associative_scan(_cplx_combine, (ar, ai, bur, bui), axis=1)
    y = (jnp.einsum('nlgp,gcp->nlgc', hr, c_re.astype(F32))
         - jnp.einsum('nlgp,gcp->nlgc', hi, c_im.astype(F32)))
    y = y.reshape(n, l, S5_WIDTH) + d_skip.astype(F32) * u.astype(F32)
    y = jax.nn.gelu(y)
    y = y * jax.nn.sigmoid(y @ w_glu.astype(F32) + b_glu.astype(F32))
    return y, hr[:, -1], hi[:, -1]


def rglru_branch(xb, gb, conv_buf, h0, conv_w, conv_b, w_r, b_r, w_i, b_i, lam):
    n, l, _ = xb.shape
    xpad = jnp.concatenate([conv_buf.astype(xb.dtype), xb], axis=1)
    xc = conv_b + xpad[:, 0:l] * conv_w[0]
    for k in range(1, CONV_WIDTH):
        xc = xc + xpad[:, k:k + l] * conv_w[k]
    new_buf = xpad[:, l:]
    xh = xc.reshape(n, l, LRU_HEADS, LRU_HEAD_DIM)
    r = jax.nn.sigmoid((jnp.einsum('nlhi,hij->nlhj', xh, w_r).reshape(n, l, LRU_WIDTH) + b_r).astype(F32))
    ig = jax.nn.sigmoid((jnp.einsum('nlhi,hij->nlhj', xh, w_i).reshape(n, l, LRU_WIDTH) + b_i).astype(F32))
    log_a = -LRU_C * r * jax.nn.softplus(-lam.astype(F32))
    a = jnp.exp(log_a)
    mult = jnp.sqrt(-jnp.expm1(2.0 * log_a))
    bterm = mult * ig * xc.astype(F32)
    bterm = bterm.at[:, 0].add(a[:, 0] * h0.astype(F32))
    _, h = lax.associative_scan(_lin_combine, (a, bterm), axis=1)
    y = jax.nn.gelu(gb.astype(F32)) * h
    return y, h[:, -1], new_buf


def peer(xn, w_q, sub_keys, u_tab, v_tab):
    n, l, d = xn.shape
    t = xn.reshape(n * l, d)
    nt = n * l
    nb = -(-nt // PEER_BLOCK)
    t = jnp.pad(t, ((0, nb * PEER_BLOCK - nt), (0, 0))).reshape(nb, PEER_BLOCK, d)

    def block(tb):
        q = (tb @ w_q).reshape(PEER_BLOCK, PEER_HEADS, 2, KEY_HALF)
        s = jnp.einsum('thsk,hsnk->thsn', q, sub_keys).astype(F32)
        sv, si = lax.top_k(s, PEER_TOPK)
        cand = sv[:, :, 0, :, None] + sv[:, :, 1, None, :]
        cidx = si[:, :, 0, :, None] * N_KEYS + si[:, :, 1, None, :]
        cand = cand.reshape(PEER_BLOCK, PEER_HEADS, PEER_TOPK * PEER_TOPK)
        cidx = cidx.reshape(PEER_BLOCK, PEER_HEADS, PEER_TOPK * PEER_TOPK)
        top_s, pos = lax.top_k(cand, PEER_TOPK)
        idx = jnp.take_along_axis(cidx, pos, axis=-1)
        g = jax.nn.softmax(top_s, axis=-1)
        u = u_tab[idx]
        act = jax.nn.gelu(jnp.einsum('thkd,td->thk', u, tb).astype(F32))
        coef = (g * act).astype(tb.dtype)
        v = v_tab[idx]
        return jnp.einsum('thk,thkd->td', coef, v)

    out = lax.map(block, t)
    return out.reshape(nb * PEER_BLOCK, d)[:nt].reshape(n, l, d)


def _layer(x, p, h_s5_re, h_s5_im, h_lru, conv_buf,
           g_mix, w_in, s5_lam_re, s5_lam_im, s5_log_dt, s5_b_re, s5_b_im, s5_c_re, s5_c_im,
           s5_d, s5_w_glu, s5_b_glu, conv_w, conv_b, lru_w_r, lru_b_r, lru_w_i, lru_b_i, lru_lam,
           w_a_proj, w_b_proj, w_out, g_ffn, peer_w_q, peer_sub_keys, peer_u, peer_v,
           g_ple, w_ple, w_ple_gate):
    xn = rms_norm(x, g_mix)
    proj = xn @ w_in
    xa, xb, gb, ga_pre, gb_pre = jnp.split(proj, SPLITS, axis=-1)
    ya, s5r_new, s5i_new = s5_branch(xa, h_s5_re, h_s5_im, s5_lam_re, s5_lam_im, s5_log_dt,
                                     s5_b_re, s5_b_im, s5_c_re, s5_c_im, s5_d, s5_w_glu, s5_b_glu)
    yb, lru_new, conv_new = rglru_branch(xb, gb, conv_buf, h_lru, conv_w, conv_b,
                                         lru_w_r, lru_b_r, lru_w_i, lru_b_i, lru_lam)
    ya_d = ya.astype(x.dtype) @ w_a_proj
    yb_d = yb.astype(x.dtype) @ w_b_proj
    merged = jax.nn.sigmoid(ga_pre) * ya_d + jax.nn.sigmoid(gb_pre) * yb_d
    x = x + (merged @ w_out).astype(x.dtype)
    x = x + peer(rms_norm(x, g_ffn), peer_w_q, peer_sub_keys, peer_u, peer_v).astype(x.dtype)
    gate = jax.nn.sigmoid(rms_norm(x, g_ple) @ w_ple_gate)
    x = x + ((p.astype(x.dtype) @ w_ple) * gate).astype(x.dtype)
    return x, s5r_new, s5i_new, lru_new, conv_new


def setup_inputs(seed: int = 0) -> dict:
    key = jax.random.key(seed)
    ks = iter(jax.random.split(key, 64))

    def nrm(shape, scale):
        return jax.random.normal(next(ks), shape, F32) * scale

    def unif(shape, lo, hi):
        return jax.random.uniform(next(ks), shape, F32, lo, hi)

    n_idx = jnp.arange(S5_STATE, dtype=F32)
    lam_re = -0.5 + nrm((DEPTH, S5_GROUPS, S5_STATE), 0.01)
    lam_im = math.pi * n_idx + nrm((DEPTH, S5_GROUPS, S5_STATE), 0.01)
    a0 = unif((DEPTH, LRU_WIDTH), 0.9, 0.999)
    sig = a0 ** (1.0 / LRU_C)
    lru_lam = jnp.log(sig) - jnp.log1p(-sig)

    inp = {
        "x_prompt": nrm((BATCH, SEQ, D_MODEL), 1.0),
        "x_sample": nrm((DEC_BATCH, DEC_SEQ, D_MODEL), 1.0),
        "state_s5_re": nrm((DEPTH, DEC_BATCH, S5_GROUPS, S5_STATE), 0.1),
        "state_s5_im": nrm((DEPTH, DEC_BATCH, S5_GROUPS, S5_STATE), 0.1),
        "state_lru": nrm((DEPTH, DEC_BATCH, LRU_WIDTH), 0.5),
        "state_conv": nrm((DEPTH, DEC_BATCH, CONV_WIDTH - 1, LRU_WIDTH), 1.0),
        "p_prompt": nrm((DEPTH, BATCH, SEQ, PLE_DIM), 1.0),
        "p_sample": nrm((DEPTH, DEC_BATCH, DEC_SEQ, PLE_DIM), 1.0),
        "g_mix": 1.0 + nrm((DEPTH, D_MODEL), 0.02),
        "w_in": nrm((DEPTH, D_MODEL, IN_COLS), D_MODEL ** -0.5),
        "s5_lam_re": lam_re,
        "s5_lam_im": lam_im,
        "s5_log_dt": unif((DEPTH, S5_GROUPS), math.log(0.001), math.log(0.1)),
        "s5_b_re": nrm((DEPTH, S5_GROUPS, S5_STATE, S5_GROUP), (2 * S5_GROUP) ** -0.5),
        "s5_b_im": nrm((DEPTH, S5_GROUPS, S5_STATE, S5_GROUP), (2 * S5_GROUP) ** -0.5),
        "s5_c_re": nrm((DEPTH, S5_GROUPS, S5_GROUP, S5_STATE), (2 * S5_STATE) ** -0.5),
        "s5_c_im": nrm((DEPTH, S5_GROUPS, S5_GROUP, S5_STATE), (2 * S5_STATE) ** -0.5),
        "s5_d": nrm((DEPTH, S5_WIDTH), 1.0),
        "s5_w_glu": nrm((DEPTH, S5_WIDTH, S5_WIDTH), S5_WIDTH ** -0.5),
        "s5_b_glu": nrm((DEPTH, S5_WIDTH), 0.01),
        "conv_w": nrm((DEPTH, CONV_WIDTH, LRU_WIDTH), CONV_WIDTH ** -0.5),
        "conv_b": nrm((DEPTH, LRU_WIDTH), 0.01),
        "lru_w_r": nrm((DEPTH, LRU_HEADS, LRU_HEAD_DIM, LRU_HEAD_DIM), LRU_HEAD_DIM ** -0.5),
        "lru_b_r": nrm((DEPTH, LRU_WIDTH), 0.01),
        "lru_w_i": nrm((DEPTH, LRU_HEADS, LRU_HEAD_DIM, LRU_HEAD_DIM), LRU_HEAD_DIM ** -0.5),
        "lru_b_i": nrm((DEPTH, LRU_WIDTH), 0.01),
        "lru_lam": lru_lam,
        "w_a_proj": nrm((DEPTH, S5_WIDTH, D_MODEL), S5_WIDTH ** -0.5),
        "w_b_proj": nrm((DEPTH, LRU_WIDTH, D_MODEL), LRU_WIDTH ** -0.5),
        "w_out": nrm((DEPTH, D_MODEL, D_MODEL), D_MODEL ** -0.5),
        "g_ffn": 1.0 + nrm((DEPTH, D_MODEL), 0.02),
        "peer_w_q": nrm((DEPTH, D_MODEL, PEER_HEADS * QUERY_DIM), D_MODEL ** -0.5),
        "peer_sub_keys": nrm((DEPTH, PEER_HEADS, 2, N_KEYS, KEY_HALF), KEY_HALF ** -0.5),
        "peer_u": nrm((DEPTH, N_EXPERTS, D_MODEL), D_MODEL ** -0.5),
        "peer_v": nrm((DEPTH, N_EXPERTS, D_MODEL), 0.5),
        "g_ple": 1.0 + nrm((DEPTH, D_MODEL), 0.02),
        "w_ple": nrm((DEPTH, PLE_DIM, D_MODEL), PLE_DIM ** -0.5),
        "w_ple_gate": nrm((DEPTH, D_MODEL, D_MODEL), D_MODEL ** -0.5),
        "g_final": 1.0 + nrm((D_MODEL,), 0.02),
    }
    return inp


def reference(x_prompt, x_sample, state_s5_re, state_s5_im, state_lru, state_conv, p_prompt, p_sample,
              g_mix, w_in, s5_lam_re, s5_lam_im, s5_log_dt, s5_b_re, s5_b_im, s5_c_re, s5_c_im,
              s5_d, s5_w_glu, s5_b_glu, conv_w, conv_b, lru_w_r, lru_b_r, lru_w_i, lru_b_i, lru_lam,
              w_a_proj, w_b_proj, w_out, g_ffn, peer_w_q, peer_sub_keys, peer_u, peer_v,
              g_ple, w_ple, w_ple_gate, g_final):
    n_p = x_prompt.shape[0]
    z_s5 = jnp.zeros((n_p, S5_GROUPS, S5_STATE), F32)
    z_lru = jnp.zeros((n_p, LRU_WIDTH), F32)
    z_conv = jnp.zeros((n_p, CONV_WIDTH - 1, LRU_WIDTH), x_prompt.dtype)
    hp, hs = x_prompt, x_sample
    sp_re, sp_im, sp_lru, sp_conv = [], [], [], []
    ss_re, ss_im, ss_lru, ss_conv = [], [], [], []
    for i in range(DEPTH):
        lw = dict(g_mix=g_mix[i], w_in=w_in[i], s5_lam_re=s5_lam_re[i], s5_lam_im=s5_lam_im[i],
                  s5_log_dt=s5_log_dt[i], s5_b_re=s5_b_re[i], s5_b_im=s5_b_im[i], s5_c_re=s5_c_re[i],
                  s5_c_im=s5_c_im[i], s5_d=s5_d[i], s5_w_glu=s5_w_glu[i], s5_b_glu=s5_b_glu[i],
                  conv_w=conv_w[i], conv_b=conv_b[i], lru_w_r=lru_w_r[i], lru_b_r=lru_b_r[i],
                  lru_w_i=lru_w_i[i], lru_b_i=lru_b_i[i], lru_lam=lru_lam[i], w_a_proj=w_a_proj[i],
                  w_b_proj=w_b_proj[i], w_out=w_out[i], g_ffn=g_ffn[i], peer_w_q=peer_w_q[i],
                  peer_sub_keys=peer_sub_keys[i], peer_u=peer_u[i], peer_v=peer_v[i],
                  g_ple=g_ple[i], w_ple=w_ple[i], w_ple_gate=w_ple_gate[i])
        hp, a_re, a_im, a_lru, a_conv = _layer(hp, p_prompt[i], z_s5, z_s5, z_lru, z_conv, **lw)
        hs, b_re, b_im, b_lru, b_conv = _layer(hs, p_sample[i], state_s5_re[i], state_s5_im[i],
                                               state_lru[i], state_conv[i], **lw)
        sp_re.append(a_re); sp_im.append(a_im); sp_lru.append(a_lru); sp_conv.append(a_conv)
        ss_re.append(b_re); ss_im.append(b_im); ss_lru.append(b_lru); ss_conv.append(b_conv)
    y_prompt = rms_norm(hp, g_final)
    y_sample = rms_norm(hs, g_final)
    new_s5_re_prompt = jnp.stack(sp_re)
    new_s5_im_prompt = jnp.stack(sp_im)
    new_lru_prompt = jnp.stack(sp_lru)
    new_conv_prompt = jnp.stack(sp_conv)
    new_s5_re_sample = jnp.stack(ss_re)
    new_s5_im_sample = jnp.stack(ss_im)
    new_lru_sample = jnp.stack(ss_lru)
    new_conv_sample = jnp.stack(ss_conv)
    return (y_prompt, y_sample, new_s5_re_prompt, new_s5_im_prompt, new_lru_prompt, new_conv_prompt,
            new_s5_re_sample, new_s5_im_sample, new_lru_sample, new_conv_sample)
```

```python
import functools

import jax
import jax.numpy as jnp
from jax import lax
from jax.experimental import pallas as pl
from jax.experimental.pallas import tpu as pltpu

F32 = jnp.float32
BF16 = jnp.bfloat16
I32 = jnp.int32

D_MODEL = 1024
S5_WIDTH = 512
S5_GROUP = 16
S5_GROUPS = 32
S5_STATE = 64
S5_HID = S5_GROUPS * S5_STATE
S5_BLOCKS = 4
LRU_WIDTH = 1024
LRU_HEADS = 8
LRU_HEAD_DIM = 128
CONV_WIDTH = 4
LRU_C = 8.0
IN_COLS = S5_WIDTH + 2 * LRU_WIDTH + 2 * D_MODEL
PEER_HEADS = 8
N_KEYS = 128
PEER_TOPK = 16
KEY_HALF = 128
N_EXPERTS = N_KEYS * N_KEYS
N_ENTRIES = PEER_HEADS * PEER_TOPK
PLE_DIM = 256
EPS = 1e-6

LANES = 128
SUBLANES = 8
VMEM_LIMIT_BYTES = 60000 * 1024

MIX_ROWS = 256
TOPK_TOKENS = 256
ACT_TOKENS = 1024
OUT_TOKENS = 256
EXPERT_CHUNK = 2048
COEF_PITCH = N_KEYS + SUBLANES
S5_SCAN_ELEMS = 4 * SUBLANES * LANES
LRU_SCAN_ELEMS = 8 * SUBLANES * LANES
SCAN_UNROLL = 4


def _rms(x, g):
    return x * lax.rsqrt(jnp.mean(x * x, axis=-1, keepdims=True) + EPS) * g


def _dot(a, b):
    return jnp.dot(a, b, preferred_element_type=F32)


def _dot_nt(a, b):
    return lax.dot_general(a, b, (((1,), (1,)), ((), ())), preferred_element_type=F32)


def _mixer_kernel(nb, tl,
                  x_ref, s5h0_ref, lruh0_ref, conv0_ref, gmix_ref, win_ref, are_ref, aim_ref,
                  bmat_ref, cmat_ref, dskip_ref, wglu_ref, bglu_ref, convw_ref, convb_ref,
                  wri_ref, br_ref, bi_ref, lam_ref, wa_ref, wb_ref, wout_ref,
                  x1_ref, s5out_ref, lruout_ref, convout_ref,
                  proj_s, bu_s, xpad_s, a_s, h_s, s5c_s, lruc_s):
    ci = pl.program_id(1)
    rows = nb * tl
    halo = (CONV_WIDTH - 1) * nb

    @pl.when(ci == 0)
    def _():
        s5c_s[...] = s5h0_ref[...]
        lruc_s[...] = lruh0_ref[...]
        xpad_s[0:halo, :] = conv0_ref[...]

    @pl.when(ci > 0)
    def _():
        xpad_s[0:halo, :] = xpad_s[rows:rows + halo, :]

    x = x_ref[...]
    xn = _rms(x, gmix_ref[...])
    proj_s[...] = _dot(xn.astype(BF16), win_ref[...])

    blk = S5_HID // S5_BLOCKS
    for b in range(S5_BLOCKS):
        ub = proj_s[:, b * LANES:(b + 1) * LANES].astype(BF16)
        bu = _dot(ub, bmat_ref[b])
        bu_s[:, b * blk:(b + 1) * blk] = bu[:, :blk]
        bu_s[:, S5_HID + b * blk:S5_HID + (b + 1) * blk] = bu[:, blk:]

    cb = min(S5_HID, S5_SCAN_ELEMS // nb)
    for c0 in range(0, S5_HID, cb):
        ar = jnp.broadcast_to(are_ref[:, c0:c0 + cb], (nb, cb))
        ai = jnp.broadcast_to(aim_ref[:, c0:c0 + cb], (nb, cb))

        def s5_step(t, carry, c0=c0, ar=ar, ai=ai):
            hr, hi = carry
            r0 = pl.multiple_of(t * nb, nb)
            bur = bu_s[pl.ds(r0, nb), c0:c0 + cb]
            bui = bu_s[pl.ds(r0, nb), S5_HID + c0:S5_HID + c0 + cb]
            nhr = ar * hr - ai * hi + bur
            nhi = ar * hi + ai * hr + bui
            bu_s[pl.ds(r0, nb), c0:c0 + cb] = nhr
            bu_s[pl.ds(r0, nb), S5_HID + c0:S5_HID + c0 + cb] = nhi
            return nhr, nhi

        hr, hi = lax.fori_loop(0, tl, s5_step,
                               (s5c_s[:, c0:c0 + cb], s5c_s[:, S5_HID + c0:S5_HID + c0 + cb]),
                               unroll=SCAN_UNROLL)
        s5c_s[:, c0:c0 + cb] = hr
        s5c_s[:, S5_HID + c0:S5_HID + c0 + cb] = hi

    ys = []
    for b in range(S5_BLOCKS):
        hre = bu_s[:, b * blk:(b + 1) * blk].astype(BF16)
        him = bu_s[:, S5_HID + b * blk:S5_HID + (b + 1) * blk].astype(BF16)
        ys.append(_dot(hre, cmat_ref[b, 0:blk, :]) + _dot(him, cmat_ref[b, blk:2 * blk, :]))
    y = jnp.concatenate(ys, axis=1) + dskip_ref[...] * proj_s[:, 0:S5_WIDTH]
    y = jax.nn.gelu(y)
    ya = y * jax.nn.sigmoid(_dot(y.astype(BF16), wglu_ref[...]) + bglu_ref[...])

    o_xb = S5_WIDTH
    o_gb = o_xb + LRU_WIDTH
    o_ga = o_gb + LRU_WIDTH
    o_gbm = o_ga + D_MODEL
    xpad_s[halo:halo + rows, :] = proj_s[:, o_xb:o_gb]
    xc = convb_ref[...] + xpad_s[0:rows, :] * convw_ref[0:1, :]
    for k in range(1, CONV_WIDTH):
        xc = xc + xpad_s[k * nb:k * nb + rows, :] * convw_ref[k:k + 1, :]
    xcb = xc.astype(BF16)
    rs, gs = [], []
    for h in range(LRU_HEADS):
        ri = _dot(xcb[:, h * LRU_HEAD_DIM:(h + 1) * LRU_HEAD_DIM], wri_ref[h])
        rs.append(ri[:, :LRU_HEAD_DIM])
        gs.append(ri[:, LRU_HEAD_DIM:])
    r = jax.nn.sigmoid(jnp.concatenate(rs, axis=1) + br_ref[...])
    ig = jax.nn.sigmoid(jnp.concatenate(gs, axis=1) + bi_ref[...])
    lam = lam_ref[...]
    softplus_neg = jnp.maximum(-lam, 0.0) + jnp.log1p(jnp.exp(-jnp.abs(lam)))
    log_a = -LRU_C * r * softplus_neg
    a = jnp.exp(log_a)
    mult = jnp.sqrt(-jnp.tanh(log_a) * (a * a + 1.0))
    a_s[...] = a
    h_s[...] = mult * ig * xc

    lb = min(LRU_WIDTH, LRU_SCAN_ELEMS // nb)
    for c0 in range(0, LRU_WIDTH, lb):
        def lru_step(t, h, c0=c0):
            r0 = pl.multiple_of(t * nb, nb)
            nh = a_s[pl.ds(r0, nb), c0:c0 + lb] * h + h_s[pl.ds(r0, nb), c0:c0 + lb]
            h_s[pl.ds(r0, nb), c0:c0 + lb] = nh
            return nh

        lruc_s[:, c0:c0 + lb] = lax.fori_loop(0, tl, lru_step, lruc_s[:, c0:c0 + lb], unroll=SCAN_UNROLL)
    yb = jax.nn.gelu(proj_s[:, o_gb:o_ga]) * h_s[...]

    ya_d = _dot(ya.astype(BF16), wa_ref[...])
    yb_d = _dot(yb.astype(BF16), wb_ref[...])
    merged = (jax.nn.sigmoid(proj_s[:, o_ga:o_gbm]) * ya_d
              + jax.nn.sigmoid(proj_s[:, o_gbm:IN_COLS]) * yb_d)
    x1_ref[...] = x + _dot(merged.astype(BF16), wout_ref[...])

    @pl.when(ci == pl.num_programs(1) - 1)
    def _():
        s5out_ref[...] = s5c_s[...]
        lruout_ref[...] = lruc_s[...]
        convout_ref[...] = xpad_s[rows:rows + halo, :]


def _mixer(x_tm, s5h0, lruh0, conv0, weights, *, groups, nb, tl, chunks):
    rows = nb * tl
    halo = (CONV_WIDTH - 1) * nb
    const = lambda shape: pl.BlockSpec(shape, lambda g, c: (0,) * len(shape))
    w_specs = [const(w.shape) for w in weights]
    in_specs = [
        pl.BlockSpec((rows, D_MODEL), lambda g, c: (g * chunks + c, 0)),
        pl.BlockSpec((nb, 2 * S5_HID), lambda g, c: (g, 0)),
        pl.BlockSpec((nb, LRU_WIDTH), lambda g, c: (g, 0)),
        pl.BlockSpec((halo, LRU_WIDTH), lambda g, c: (g, 0)),
    ] + w_specs
    out_specs = [
        pl.BlockSpec((rows, D_MODEL), lambda g, c: (g * chunks + c, 0)),
        pl.BlockSpec((nb, 2 * S5_HID), lambda g, c: (g, 0)),
        pl.BlockSpec((nb, LRU_WIDTH), lambda g, c: (g, 0)),
        pl.BlockSpec((halo, LRU_WIDTH), lambda g, c: (g, 0)),
    ]
    out_shape = [
        jax.ShapeDtypeStruct(x_tm.shape, F32),
        jax.ShapeDtypeStruct((groups * nb, 2 * S5_HID), F32),
        jax.ShapeDtypeStruct((groups * nb, LRU_WIDTH), F32),
        jax.ShapeDtypeStruct((groups * halo, LRU_WIDTH), F32),
    ]
    scratch = [
        pltpu.VMEM((rows, IN_COLS), F32),
        pltpu.VMEM((rows, 2 * S5_HID), F32),
        pltpu.VMEM((rows + halo, LRU_WIDTH), F32),
        pltpu.VMEM((rows, LRU_WIDTH), F32),
        pltpu.VMEM((rows, LRU_WIDTH), F32),
        pltpu.VMEM((nb, 2 * S5_HID), F32),
        pltpu.VMEM((nb, LRU_WIDTH), F32),
    ]
    return pl.pallas_call(
        functools.partial(_mixer_kernel, nb, tl),
        grid=(groups, chunks),
        in_specs=in_specs, out_specs=out_specs, out_shape=out_shape, scratch_shapes=scratch,
        compiler_params=pltpu.CompilerParams(
            dimension_semantics=("arbitrary", "arbitrary"), vmem_limit_bytes=VMEM_LIMIT_BYTES),
        name="mixer",
    )(x_tm, s5h0, lruh0, conv0, *weights)


def _extract_top(vals, ids, n_out):
    w = vals.shape[1]
    krow = lax.broadcasted_iota(I32, (n_out, w), 0)
    out_v = jnp.zeros((n_out, w), F32)
    out_i = jnp.zeros((n_out, w), I32)
    big = jnp.int32(2 ** 30)
    for k in range(n_out):
        m = jnp.max(vals, axis=0, keepdims=True)
        sel = jnp.min(jnp.where(vals == m, ids, big), axis=0, keepdims=True)
        vals = jnp.where(ids == sel, -jnp.inf, vals)
        out_v = jnp.where(krow == k, m, out_v)
        out_i = jnp.where(krow == k, sel, out_i)
    return out_v, out_i


def _pick_rows(table, sel):
    out = jnp.zeros(sel.shape, table.dtype)
    for r in range(table.shape[0]):
        out = jnp.where(sel == r, table[r:r + 1, :], out)
    return out


def _topk_kernel(x1_ref, gffn_ref, wqt_ref, keys_ref, tn_ref, i_ref, j_ref, g_ref,
                 qt_s, sv_s, si_s):
    tokens = x1_ref.shape[0]
    tn = _rms(x1_ref[...], gffn_ref[...]).astype(BF16)
    tn_ref[...] = tn
    qt_s[...] = _dot_nt(wqt_ref[...], tn).astype(BF16)

    def stage1(hs, carry):
        q = qt_s[pl.ds(pl.multiple_of(hs * KEY_HALF, KEY_HALF), KEY_HALF), :]
        s = _dot(keys_ref[hs], q)
        for c0 in range(0, tokens, LANES):
            ids = lax.broadcasted_iota(I32, (N_KEYS, LANES), 0)
            v, i = _extract_top(s[:, c0:c0 + LANES], ids, PEER_TOPK)
            sv_s[hs, :, c0:c0 + LANES] = v
            si_s[hs, :, c0:c0 + LANES] = i
        return carry

    lax.fori_loop(0, 2 * PEER_HEADS, stage1, 0)

    def stage2(h, carry):
        for c0 in range(0, tokens, LANES):
            sv0 = sv_s[2 * h, :, c0:c0 + LANES]
            sv1 = sv_s[2 * h + 1, :, c0:c0 + LANES]
            row = lax.broadcasted_iota(I32, (SUBLANES, LANES), 0)
            cand, pos = [], []
            for k1 in range(SUBLANES):
                n0 = PEER_TOPK // (k1 + 1)
                for base in range(0, n0, SUBLANES):
                    v = sv0[base:base + SUBLANES, :] + sv1[k1:k1 + 1, :]
                    cand.append(jnp.where(row + base < n0, v, -jnp.inf))
                    pos.append((row + base) * PEER_TOPK + k1)
            cand.append(sv0[0:1, :] + sv1[SUBLANES:2 * SUBLANES, :])
            pos.append(row + SUBLANES)
            top_s, top_p = _extract_top(jnp.concatenate(cand, axis=0), jnp.concatenate(pos, axis=0),
                                        PEER_TOPK)
            k0 = lax.shift_right_logical(top_p, 4)
            k1 = jnp.bitwise_and(top_p, PEER_TOPK - 1)
            e = jnp.exp(top_s - top_s[0:1, :])
            rows = pl.ds(pl.multiple_of(h * PEER_TOPK, PEER_TOPK), PEER_TOPK)
            i_ref[rows, c0:c0 + LANES] = _pick_rows(si_s[2 * h, :, c0:c0 + LANES], k0)
            j_ref[rows, c0:c0 + LANES] = _pick_rows(si_s[2 * h + 1, :, c0:c0 + LANES], k1)
            g_ref[rows, c0:c0 + LANES] = e / jnp.sum(e, axis=0, keepdims=True)
        return carry

    lax.fori_loop(0, PEER_HEADS, stage2, 0)


def _topk(x1, gffn, wqt, keys):
    t = x1.shape[0]
    tb = TOPK_TOKENS
    const = lambda shape: pl.BlockSpec(shape, lambda i: (0,) * len(shape))
    return pl.pallas_call(
        _topk_kernel,
        grid=(t // tb,),
        in_specs=[pl.BlockSpec((tb, D_MODEL), lambda i: (i, 0)), const(gffn.shape), const(wqt.shape),
                  const(keys.shape)],
        out_specs=[pl.BlockSpec((tb, D_MODEL), lambda i: (i, 0))]
        + [pl.BlockSpec((N_ENTRIES, tb), lambda i: (0, i))] * 3,
        out_shape=[jax.ShapeDtypeStruct((t, D_MODEL), BF16),
                   jax.ShapeDtypeStruct((N_ENTRIES, t), I32),
                   jax.ShapeDtypeStruct((N_ENTRIES, t), I32),
                   jax.ShapeDtypeStruct((N_ENTRIES, t), F32)],
        scratch_shapes=[pltpu.VMEM((2 * PEER_HEADS * KEY_HALF, tb), BF16),
                        pltpu.VMEM((2 * PEER_HEADS, PEER_TOPK, tb), F32),
                        pltpu.VMEM((2 * PEER_HEADS, PEER_TOPK, tb), I32)],
        compiler_params=pltpu.CompilerParams(
            dimension_semantics=("arbitrary",), vmem_limit_bytes=VMEM_LIMIT_BYTES),
        name="topk",
    )(x1, gffn, wqt, keys)


ACT_ROWS = 32


def _expert_act_kernel(tn_ref, i_ref, j_ref, g_ref, ut_ref, c_ref, act_s, sel_s):
    c = pl.program_id(1)
    tokens = tn_ref.shape[0]
    chunk = ut_ref.shape[1]

    @pl.when(c == 0)
    def _():
        sel_s[...] = jnp.zeros_like(sel_s)

    act_s[...] = _dot(tn_ref[...], ut_ref[...])
    i_base = c * (chunk // N_KEYS)

    def rows(rb, carry):
        r0 = pl.multiple_of(rb * ACT_ROWS, ACT_ROWS)
        ii = i_ref[pl.ds(r0, ACT_ROWS), :]
        jj = j_ref[pl.ds(r0, ACT_ROWS), :]
        sel = sel_s[pl.ds(r0, ACT_ROWS), :]
        for k in range(chunk // N_KEYS):
            a = act_s[pl.ds(r0, ACT_ROWS), k * N_KEYS:(k + 1) * N_KEYS]
            sel = jnp.where(ii == i_base + k, jnp.take_along_axis(a, jj, axis=1), sel)
        sel_s[pl.ds(r0, ACT_ROWS), :] = sel
        return carry

    lax.fori_loop(0, tokens // ACT_ROWS, rows, 0)

    @pl.when(c == pl.num_programs(1) - 1)
    def _():
        c_ref[...] = g_ref[...] * jax.nn.gelu(sel_s[...])


def _expert_act(tn, idx_i, idx_j, gate, ut):
    t = tn.shape[0]
    tb = ACT_TOKENS
    tok = lambda w: pl.BlockSpec((tb, w), lambda i, c: (i, 0))
    return pl.pallas_call(
        _expert_act_kernel,
        grid=(t // tb, N_EXPERTS // EXPERT_CHUNK),
        in_specs=[tok(D_MODEL), tok(N_ENTRIES), tok(N_ENTRIES), tok(N_ENTRIES),
                  pl.BlockSpec((D_MODEL, EXPERT_CHUNK), lambda i, c: (0, c))],
        out_specs=tok(N_ENTRIES),
        out_shape=jax.ShapeDtypeStruct((t, N_ENTRIES), F32),
        scratch_shapes=[pltpu.VMEM((tb, EXPERT_CHUNK), F32), pltpu.VMEM((tb, N_ENTRIES), F32)],
        compiler_params=pltpu.CompilerParams(
            dimension_semantics=("arbitrary", "arbitrary"), vmem_limit_bytes=VMEM_LIMIT_BYTES),
        name="expert_act",
    )(tn, idx_i, idx_j, gate, ut)


def _expert_out_kernel(i_ref, j_ref, c_ref, v_ref, x1_ref, p_ref, gple_ref, wple_ref, wgate_ref,
                       gfin_ref, y_ref, coef_s, acc_s):
    c = pl.program_id(1)
    tokens = x1_ref.shape[0]
    chunk = v_ref.shape[0]

    @pl.when(c == 0)
    def _():
        acc_s[...] = jnp.zeros_like(acc_s)

        def token(t, carry):
            sub = lax.broadcasted_iota(I32, (N_KEYS, N_ENTRIES), 0)
            pt = jnp.where(sub == i_ref[pl.ds(t, 1), :], c_ref[pl.ds(t, 1), :], 0.0).astype(BF16)
            qt = jnp.where(sub == j_ref[pl.ds(t, 1), :], 1.0, 0.0).astype(BF16)
            coef_s[pl.ds(pl.multiple_of(t * COEF_PITCH, SUBLANES), N_KEYS), :] = _dot_nt(pt, qt)
            return carry

        lax.fori_loop(0, tokens, token, 0)

    i_base = c * (chunk // N_KEYS)
    lhs = [coef_s[pl.ds(i_base + k, tokens, stride=COEF_PITCH), :].astype(BF16)
           for k in range(chunk // N_KEYS)]
    acc_s[...] += _dot(jnp.concatenate(lhs, axis=1), v_ref[...])

    @pl.when(c == pl.num_programs(1) - 1)
    def _():
        x2 = x1_ref[...] + acc_s[...]
        gate = jax.nn.sigmoid(_dot(_rms(x2, gple_ref[...]).astype(BF16), wgate_ref[...]))
        x3 = x2 + _dot(p_ref[...].astype(BF16), wple_ref[...]) * gate
        y_ref[...] = _rms(x3, gfin_ref[...])


def _expert_out(idx_i, idx_j, coef, vtab, x1, p, gple, wple, wgate, gfin):
    t = x1.shape[0]
    tb = OUT_TOKENS
    tok = lambda w: pl.BlockSpec((tb, w), lambda i, c: (i, 0))
    const = lambda shape: pl.BlockSpec(shape, lambda i, c: (0,) * len(shape))
    return pl.pallas_call(
        _expert_out_kernel,
        grid=(t // tb, N_EXPERTS // EXPERT_CHUNK),
        in_specs=[tok(N_ENTRIES), tok(N_ENTRIES), tok(N_ENTRIES),
                  pl.BlockSpec((EXPERT_CHUNK, D_MODEL), lambda i, c: (c, 0)),
                  tok(D_MODEL), tok(PLE_DIM), const(gple.shape), const(wple.shape), const(wgate.shape),
                  const(gfin.shape)],
        out_specs=tok(D_MODEL),
        out_shape=jax.ShapeDtypeStruct((t, D_MODEL), F32),
        scratch_shapes=[pltpu.VMEM((tb * COEF_PITCH, N_KEYS), F32), pltpu.VMEM((tb, D_MODEL), F32)],
        compiler_params=pltpu.CompilerParams(
            dimension_semantics=("arbitrary", "arbitrary"), vmem_limit_bytes=VMEM_LIMIT_BYTES),
        name="expert_out",
    )(idx_i, idx_j, coef, vtab, x1, p, gple, wple, wgate, gfin)


def _s5_params(lam_re, lam_im, log_dt, b_re, b_im, c_re, c_im):
    dt = jnp.exp(log_dt)[:, None]
    mag = jnp.exp(lam_re * dt)
    abr, abi = mag * jnp.cos(lam_im * dt), mag * jnp.sin(lam_im * dt)
    den = lam_re * lam_re + lam_im * lam_im
    nr, ni = abr - 1.0, abi
    coef_r = (nr * lam_re + ni * lam_im) / den
    coef_i = (ni * lam_re - nr * lam_im) / den
    bbr = coef_r[..., None] * b_re - coef_i[..., None] * b_im
    bbi = coef_r[..., None] * b_im + coef_i[..., None] * b_re
    gpb = S5_GROUPS // S5_BLOCKS
    eye = jnp.eye(gpb, dtype=F32)

    def b_blocks(bb):
        bb = bb.reshape(S5_BLOCKS, gpb, S5_STATE, S5_GROUP)
        return jnp.einsum('bgpc,gh->bgchp', bb, eye).reshape(S5_BLOCKS, gpb * S5_GROUP, gpb * S5_STATE)

    def c_blocks(cc):
        cc = cc.reshape(S5_BLOCKS, gpb, S5_GROUP, S5_STATE)
        return jnp.einsum('bgcp,gh->bgphc', cc, eye).reshape(S5_BLOCKS, gpb * S5_STATE, gpb * S5_GROUP)

    bmat = jnp.concatenate([b_blocks(bbr), b_blocks(bbi)], axis=2).astype(BF16)
    cmat = jnp.concatenate([c_blocks(c_re), -c_blocks(c_im)], axis=1).astype(BF16)
    return abr.reshape(1, S5_HID), abi.reshape(1, S5_HID), bmat, cmat


def kernel(x_prompt, x_sample, state_s5_re, state_s5_im, state_lru, state_conv, p_prompt, p_sample, g_mix, w_in, s5_lam_re, s5_lam_im, s5_log_dt, s5_b_re, s5_b_im, s5_c_re, s5_c_im, s5_d, s5_w_glu, s5_b_glu, conv_w, conv_b, lru_w_r, lru_b_r, lru_w_i, lru_b_i, lru_lam, w_a_proj, w_b_proj, w_out, g_ffn, peer_w_q, peer_sub_keys, peer_u, peer_v, g_ple, w_ple, w_ple_gate, g_final):
    depth = g_mix.shape[0]
    assert depth == 1, "one layer per call"
    n_p, l_p, _ = x_prompt.shape
    n_s, l_s, _ = x_sample.shape
    row = lambda v: v.reshape(1, -1)

    a_re, a_im, bmat, cmat = _s5_params(s5_lam_re[0], s5_lam_im[0], s5_log_dt[0], s5_b_re[0], s5_b_im[0],
                                        s5_c_re[0], s5_c_im[0])
    w_ri = jnp.concatenate([lru_w_r[0], lru_w_i[0]], axis=2).astype(BF16)
    mixer_w = (row(g_mix[0]), w_in[0].astype(BF16), a_re, a_im, bmat, cmat, row(s5_d[0]),
               s5_w_glu[0].astype(BF16), row(s5_b_glu[0]), conv_w[0], row(conv_b[0]), w_ri,
               row(lru_b_r[0]), row(lru_b_i[0]), row(lru_lam[0]), w_a_proj[0].astype(BF16),
               w_b_proj[0].astype(BF16), w_out[0].astype(BF16))

    tl_p = MIX_ROWS // n_p
    xp = x_prompt.transpose(1, 0, 2).reshape(l_p * n_p, D_MODEL)
    halo_p = (CONV_WIDTH - 1) * n_p
    x1_p, s5_p, lru_p, conv_p = _mixer(
        xp, jnp.zeros((n_p, 2 * S5_HID), F32), jnp.zeros((n_p, LRU_WIDTH), F32),
        jnp.zeros((halo_p, LRU_WIDTH), F32), mixer_w, groups=1, nb=n_p, tl=tl_p, chunks=l_p // tl_p)

    nb_s = MIX_ROWS // l_s
    grp = n_s // nb_s
    xs = x_sample.reshape(grp, nb_s, l_s, D_MODEL).transpose(0, 2, 1, 3).reshape(n_s * l_s, D_MODEL)
    s5_0 = jnp.concatenate([state_s5_re[0].reshape(n_s, S5_HID), state_s5_im[0].reshape(n_s, S5_HID)], axis=1)
    conv_0 = (state_conv[0].reshape(grp, nb_s, CONV_WIDTH - 1, LRU_WIDTH).transpose(0, 2, 1, 3)
              .reshape(grp * (CONV_WIDTH - 1) * nb_s, LRU_WIDTH))
    x1_s, s5_s, lru_s, conv_s = _mixer(xs, s5_0, state_lru[0], conv_0, mixer_w,
                                       groups=grp, nb=nb_s, tl=l_s, chunks=1)

    x1 = jnp.concatenate([x1_p, x1_s], axis=0)
    pp = jnp.concatenate([
        p_prompt[0].transpose(1, 0, 2).reshape(l_p * n_p, PLE_DIM),
        p_sample[0].reshape(grp, nb_s, l_s, PLE_DIM).transpose(0, 2, 1, 3).reshape(n_s * l_s, PLE_DIM)], axis=0)
    keys = peer_sub_keys[0].reshape(2 * PEER_HEADS, N_KEYS, KEY_HALF).astype(BF16)
    tn, idx_i, idx_j, gate = _topk(x1, row(g_ffn[0]), peer_w_q[0].T.astype(BF16), keys)
    idx_i, idx_j, gate = idx_i.T, idx_j.T, gate.T
    coef = _expert_act(tn, idx_i, idx_j, gate, peer_u[0].T.astype(BF16))
    y = _expert_out(idx_i, idx_j, coef, peer_v[0].astype(BF16), x1, pp, row(g_ple[0]),
                    w_ple[0].astype(BF16), w_ple_gate[0].astype(BF16), row(g_final))

    t_p = l_p * n_p
    y_prompt = y[:t_p].reshape(l_p, n_p, D_MODEL).transpose(1, 0, 2)
    y_sample = y[t_p:].reshape(grp, l_s, nb_s, D_MODEL).transpose(0, 2, 1, 3).reshape(n_s, l_s, D_MODEL)

    def states(s5, lru, conv, n, groups, nb):
        re = s5[:, :S5_HID].reshape(1, n, S5_GROUPS, S5_STATE)
        im = s5[:, S5_HID:].reshape(1, n, S5_GROUPS, S5_STATE)
        cv = (conv.reshape(groups, CONV_WIDTH - 1, nb, LRU_WIDTH).transpose(0, 2, 1, 3)
              .reshape(1, n, CONV_WIDTH - 1, LRU_WIDTH))
        return re, im, lru.reshape(1, n, LRU_WIDTH), cv

    sp = states(s5_p, lru_p, conv_p, n_p, 1, n_p)
    ss = states(s5_s, lru_s, conv_s, n_s, grp, nb_s)
    return (y_prompt, y_sample) + sp + ss
```

```python
import functools

import jax
import jax.numpy as jnp
from jax import lax
from jax.experimental import pallas as pl
from jax.experimental.pallas import tpu as pltpu

F32 = jnp.float32
BF16 = jnp.bfloat16
I32 = jnp.int32

D_MODEL = 1024
S5_WIDTH = 512
S5_GROUP = 16
S5_GROUPS = 32
S5_STATE = 64
S5_HID = S5_GROUPS * S5_STATE
S5_BLOCKS = 4
LRU_WIDTH = 1024
LRU_HEADS = 8
LRU_HEAD_DIM = 128
CONV_WIDTH = 4
LRU_C = 8.0
IN_COLS = S5_WIDTH + 2 * LRU_WIDTH + 2 * D_MODEL
PEER_HEADS = 8
N_KEYS = 128
PEER_TOPK = 16
KEY_HALF = 128
N_EXPERTS = N_KEYS * N_KEYS
N_ENTRIES = PEER_HEADS * PEER_TOPK
PLE_DIM = 256
EPS = 1e-6

LANES = 128
SUBLANES = 8
VMEM_LIMIT_BYTES = 60000 * 1024

MIX_ROWS = 256
TOPK_TOKENS = 512
ACT_TOKENS = 1024
OUT_TOKENS = 256
EXPERT_CHUNK = 2048
COEF_PITCH = N_KEYS + SUBLANES
S5_SCAN_ELEMS = 4 * SUBLANES * LANES
LRU_SCAN_ELEMS = 8 * SUBLANES * LANES
SCAN_UNROLL = 4
SCATTER_UNROLL = 16
GATHER_UNROLL = 8


def _rms(x, g):
    return x * lax.rsqrt(jnp.mean(x * x, axis=-1, keepdims=True) + EPS) * g


def _resident(shape):
    return pl.BlockSpec(shape, lambda *_: (0,) * len(shape), pipeline_mode=pl.Buffered(1))


def _dot(a, b):
    return jnp.dot(a, b, preferred_element_type=F32)


def _dot_nt(a, b):
    return lax.dot_general(a, b, (((1,), (1,)), ((), ())), preferred_element_type=F32)


def _mixer_kernel(nb, tl,
                  x_ref, s5h0_ref, lruh0_ref, conv0_ref, gmix_ref, win_ref, are_ref, aim_ref,
                  bmat_ref, cmat_ref, dskip_ref, wglu_ref, bglu_ref, convw_ref, convb_ref,
                  wri_ref, br_ref, bi_ref, lam_ref, wa_ref, wb_ref, wout_ref,
                  x1_ref, s5out_ref, lruout_ref, convout_ref,
                  proj_s, bu_s, xpad_s, a_s, h_s, s5c_s, lruc_s):
    ci = pl.program_id(1)
    rows = nb * tl
    halo = (CONV_WIDTH - 1) * nb

    @pl.when(ci == 0)
    def _():
        s5c_s[...] = s5h0_ref[...]
        lruc_s[...] = lruh0_ref[...]
        xpad_s[0:halo, :] = conv0_ref[...]

    @pl.when(ci > 0)
    def _():
        xpad_s[0:halo, :] = xpad_s[rows:rows + halo, :]

    x = x_ref[...]
    xn = _rms(x, gmix_ref[...])
    proj_s[...] = _dot(xn.astype(BF16), win_ref[...])

    blk = S5_HID // S5_BLOCKS
    for b in range(S5_BLOCKS):
        ub = proj_s[:, b * LANES:(b + 1) * LANES].astype(BF16)
        bu = _dot(ub, bmat_ref[b])
        bu_s[:, b * blk:(b + 1) * blk] = bu[:, :blk]
        bu_s[:, S5_HID + b * blk:S5_HID + (b + 1) * blk] = bu[:, blk:]

    cb = max(LANES, S5_SCAN_ELEMS // nb)
    for c0 in range(0, S5_HID, cb):
        ar = jnp.broadcast_to(are_ref[:, c0:c0 + cb], (nb, cb))
        ai = jnp.broadcast_to(aim_ref[:, c0:c0 + cb], (nb, cb))

        def s5_step(t, carry, c0=c0, ar=ar, ai=ai):
            hr, hi = carry
            r0 = pl.multiple_of(t * nb, nb)
            bur = bu_s[pl.ds(r0, nb), c0:c0 + cb]
            bui = bu_s[pl.ds(r0, nb), S5_HID + c0:S5_HID + c0 + cb]
            nhr = ar * hr - ai * hi + bur
            nhi = ar * hi + ai * hr + bui
            bu_s[pl.ds(r0, nb), c0:c0 + cb] = nhr
            bu_s[pl.ds(r0, nb), S5_HID + c0:S5_HID + c0 + cb] = nhi
            return nhr, nhi

        hr, hi = lax.fori_loop(0, tl, s5_step,
                               (s5c_s[:, c0:c0 + cb], s5c_s[:, S5_HID + c0:S5_HID + c0 + cb]),
                               unroll=SCAN_UNROLL)
        s5c_s[:, c0:c0 + cb] = hr
        s5c_s[:, S5_HID + c0:S5_HID + c0 + cb] = hi

    ys = []
    for b in range(S5_BLOCKS):
        hre = bu_s[:, b * blk:(b + 1) * blk].astype(BF16)
        him = bu_s[:, S5_HID + b * blk:S5_HID + (b + 1) * blk].astype(BF16)
        ys.append(_dot(hre, cmat_ref[b, 0:blk, :]) + _dot(him, cmat_ref[b, blk:2 * blk, :]))
    y = jnp.concatenate(ys, axis=1) + dskip_ref[...] * proj_s[:, 0:S5_WIDTH]
    y = jax.nn.gelu(y)
    ya = y * jax.nn.sigmoid(_dot(y.astype(BF16), wglu_ref[...]) + bglu_ref[...])

    o_xb = S5_WIDTH
    o_gb = o_xb + LRU_WIDTH
    o_ga = o_gb + LRU_WIDTH
    o_gbm = o_ga + D_MODEL
    xpad_s[halo:halo + rows, :] = proj_s[:, o_xb:o_gb]
    xc = convb_ref[...] + xpad_s[0:rows, :] * convw_ref[0:1, :]
    for k in range(1, CONV_WIDTH):
        xc = xc + xpad_s[k * nb:k * nb + rows, :] * convw_ref[k:k + 1, :]
    xcb = xc.astype(BF16)
    rs, gs = [], []
    for h in range(LRU_HEADS):
        ri = _dot(xcb[:, h * LRU_HEAD_DIM:(h + 1) * LRU_HEAD_DIM], wri_ref[h])
        rs.append(ri[:, :LRU_HEAD_DIM])
        gs.append(ri[:, LRU_HEAD_DIM:])
    r = jax.nn.sigmoid(jnp.concatenate(rs, axis=1) + br_ref[...])
    ig = jax.nn.sigmoid(jnp.concatenate(gs, axis=1) + bi_ref[...])
    lam = lam_ref[...]
    softplus_neg = jnp.maximum(-lam, 0.0) + jnp.log1p(jnp.exp(-jnp.abs(lam)))
    log_a = -LRU_C * r * softplus_neg
    a = jnp.exp(log_a)
    mult = jnp.sqrt(-jnp.tanh(log_a) * (a * a + 1.0))
    a_s[...] = a
    h_s[...] = mult * ig * xc

    lb = max(LANES, LRU_SCAN_ELEMS // nb)
    for c0 in range(0, LRU_WIDTH, lb):
        def lru_step(t, h, c0=c0):
            r0 = pl.multiple_of(t * nb, nb)
            nh = a_s[pl.ds(r0, nb), c0:c0 + lb] * h + h_s[pl.ds(r0, nb), c0:c0 + lb]
            h_s[pl.ds(r0, nb), c0:c0 + lb] = nh
            return nh

        lruc_s[:, c0:c0 + lb] = lax.fori_loop(0, tl, lru_step, lruc_s[:, c0:c0 + lb], unroll=SCAN_UNROLL)
    yb = jax.nn.gelu(proj_s[:, o_gb:o_ga]) * h_s[...]

    ya_d = _dot(ya.astype(BF16), wa_ref[...])
    yb_d = _dot(yb.astype(BF16), wb_ref[...])
    merged = (jax.nn.sigmoid(proj_s[:, o_ga:o_gbm]) * ya_d
              + jax.nn.sigmoid(proj_s[:, o_gbm:IN_COLS]) * yb_d)
    x1_ref[...] = x + _dot(merged.astype(BF16), wout_ref[...])

    @pl.when(ci == pl.num_programs(1) - 1)
    def _():
        s5out_ref[...] = s5c_s[...]
        lruout_ref[...] = lruc_s[...]
        convout_ref[...] = xpad_s[rows:rows + halo, :]


def _mixer(x_tm, s5h0, lruh0, conv0, weights, *, groups, nb, tl, chunks):
    rows = nb * tl
    halo = (CONV_WIDTH - 1) * nb
    w_specs = [_resident(w.shape) for w in weights]
    in_specs = [
        pl.BlockSpec((rows, D_MODEL), lambda g, c: (g * chunks + c, 0)),
        pl.BlockSpec((nb, 2 * S5_HID), lambda g, c: (g, 0)),
        pl.BlockSpec((nb, LRU_WIDTH), lambda g, c: (g, 0)),
        pl.BlockSpec((halo, LRU_WIDTH), lambda g, c: (g, 0)),
    ] + w_specs
    out_specs = [
        pl.BlockSpec((rows, D_MODEL), lambda g, c: (g * chunks + c, 0)),
        pl.BlockSpec((nb, 2 * S5_HID), lambda g, c: (g, 0)),
        pl.BlockSpec((nb, LRU_WIDTH), lambda g, c: (g, 0)),
        pl.BlockSpec((halo, LRU_WIDTH), lambda g, c: (g, 0)),
    ]
    out_shape = [
        jax.ShapeDtypeStruct(x_tm.shape, F32),
        jax.ShapeDtypeStruct((groups * nb, 2 * S5_HID), F32),
        jax.ShapeDtypeStruct((groups * nb, LRU_WIDTH), F32),
        jax.ShapeDtypeStruct((groups * halo, LRU_WIDTH), F32),
    ]
    scratch = [
        pltpu.VMEM((rows, IN_COLS), F32),
        pltpu.VMEM((rows, 2 * S5_HID), F32),
        pltpu.VMEM((rows + halo, LRU_WIDTH), F32),
        pltpu.VMEM((rows, LRU_WIDTH), F32),
        pltpu.VMEM((rows, LRU_WIDTH), F32),
        pltpu.VMEM((nb, 2 * S5_HID), F32),
        pltpu.VMEM((nb, LRU_WIDTH), F32),
    ]
    return pl.pallas_call(
        functools.partial(_mixer_kernel, nb, tl),
        grid=(groups, chunks),
        in_specs=in_specs, out_specs=out_specs, out_shape=out_shape, scratch_shapes=scratch,
        compiler_params=pltpu.CompilerParams(
            dimension_semantics=("arbitrary", "arbitrary"), vmem_limit_bytes=VMEM_LIMIT_BYTES),
        name="mixer",
    )(x_tm, s5h0, lruh0, conv0, *weights)


def _extract_top(vals, ids, n_out):
    w = vals.shape[1]
    krow = lax.broadcasted_iota(I32, (n_out, w), 0)
    out_v = jnp.zeros((n_out, w), F32)
    out_i = jnp.zeros((n_out, w), I32)
    big = jnp.int32(2 ** 30)
    for k in range(n_out):
        m = jnp.max(vals, axis=0, keepdims=True)
        sel = jnp.min(jnp.where(vals == m, ids, big), axis=0, keepdims=True)
        vals = jnp.where(ids == sel, -jnp.inf, vals)
        out_v = jnp.where(krow == k, m, out_v)
        out_i = jnp.where(krow == k, sel, out_i)
    return out_v, out_i


def _pick_rows(table, sel):
    out = jnp.zeros(sel.shape, table.dtype)
    for r in range(table.shape[0]):
        out = jnp.where(sel == r, table[r:r + 1, :], out)
    return out


def _topk_kernel(x1_ref, gffn_ref, wqt_ref, keys_ref, tn_ref, i_ref, j_ref, g_ref,
                 qt_s, sv_s, si_s):
    tokens = x1_ref.shape[0]
    tn = _rms(x1_ref[...], gffn_ref[...]).astype(BF16)
    tn_ref[...] = tn
    qt_s[...] = _dot_nt(wqt_ref[...], tn).astype(BF16)

    def stage1(hs, carry):
        q = qt_s[pl.ds(pl.multiple_of(hs * KEY_HALF, KEY_HALF), KEY_HALF), :]
        s = _dot(keys_ref[hs], q)
        for c0 in range(0, tokens, LANES):
            ids = lax.broadcasted_iota(I32, (N_KEYS, LANES), 0)
            v, i = _extract_top(s[:, c0:c0 + LANES], ids, PEER_TOPK)
            sv_s[hs, :, c0:c0 + LANES] = v
            si_s[hs, :, c0:c0 + LANES] = i
        return carry

    lax.fori_loop(0, 2 * PEER_HEADS, stage1, 0)

    def stage2(h, carry):
        for c0 in range(0, tokens, LANES):
            sv0 = sv_s[2 * h, :, c0:c0 + LANES]
            sv1 = sv_s[2 * h + 1, :, c0:c0 + LANES]
            row = lax.broadcasted_iota(I32, (SUBLANES, LANES), 0)
            cand, pos = [], []
            for k1 in range(SUBLANES):
                n0 = PEER_TOPK // (k1 + 1)
                for base in range(0, n0, SUBLANES):
                    v = sv0[base:base + SUBLANES, :] + sv1[k1:k1 + 1, :]
                    cand.append(jnp.where(row + base < n0, v, -jnp.inf))
                    pos.append((row + base) * PEER_TOPK + k1)
            cand.append(sv0[0:1, :] + sv1[SUBLANES:2 * SUBLANES, :])
            pos.append(row + SUBLANES)
            top_s, top_p = _extract_top(jnp.concatenate(cand, axis=0), jnp.concatenate(pos, axis=0),
                                        PEER_TOPK)
            k0 = lax.shift_right_logical(top_p, 4)
            k1 = jnp.bitwise_and(top_p, PEER_TOPK - 1)
            e = jnp.exp(top_s - top_s[0:1, :])
            rows = pl.ds(pl.multiple_of(h * PEER_TOPK, PEER_TOPK), PEER_TOPK)
            i_ref[rows, c0:c0 + LANES] = _pick_rows(si_s[2 * h, :, c0:c0 + LANES], k0)
            j_ref[rows, c0:c0 + LANES] = _pick_rows(si_s[2 * h + 1, :, c0:c0 + LANES], k1)
            g_ref[rows, c0:c0 + LANES] = e / jnp.sum(e, axis=0, keepdims=True)
        return carry

    lax.fori_loop(0, PEER_HEADS, stage2, 0)


def _topk(x1, gffn, wqt, keys):
    t = x1.shape[0]
    tb = TOPK_TOKENS
    return pl.pallas_call(
        _topk_kernel,
        grid=(t // tb,),
        in_specs=[pl.BlockSpec((tb, D_MODEL), lambda i: (i, 0)), _resident(gffn.shape),
                  _resident(wqt.shape), _resident(keys.shape)],
        out_specs=[pl.BlockSpec((tb, D_MODEL), lambda i: (i, 0))]
        + [pl.BlockSpec((N_ENTRIES, tb), lambda i: (0, i))] * 3,
        out_shape=[jax.ShapeDtypeStruct((t, D_MODEL), BF16),
                   jax.ShapeDtypeStruct((N_ENTRIES, t), I32),
                   jax.ShapeDtypeStruct((N_ENTRIES, t), I32),
                   jax.ShapeDtypeStruct((N_ENTRIES, t), F32)],
        scratch_shapes=[pltpu.VMEM((2 * PEER_HEADS * KEY_HALF, tb), BF16),
                        pltpu.VMEM((2 * PEER_HEADS, PEER_TOPK, tb), F32),
                        pltpu.VMEM((2 * PEER_HEADS, PEER_TOPK, tb), I32)],
        compiler_params=pltpu.CompilerParams(
            dimension_semantics=("arbitrary",), vmem_limit_bytes=VMEM_LIMIT_BYTES),
        name="topk",
    )(x1, gffn, wqt, keys)


ACT_ROWS = SUBLANES


def _expert_act_kernel(tn_ref, i_ref, j_ref, g_ref, ut_ref, c_ref, act_s, sel_s):
    c = pl.program_id(1)
    tokens = tn_ref.shape[0]
    chunk = ut_ref.shape[1]

    @pl.when(c == 0)
    def _():
        sel_s[...] = jnp.zeros_like(sel_s)

    act_s[...] = _dot(tn_ref[...], ut_ref[...])
    i_base = c * (chunk // N_KEYS)

    def rows(rb, carry):
        r0 = pl.multiple_of(rb * ACT_ROWS, ACT_ROWS)
        ii = i_ref[pl.ds(r0, ACT_ROWS), :]
        jj = j_ref[pl.ds(r0, ACT_ROWS), :]
        sel = sel_s[pl.ds(r0, ACT_ROWS), :]
        for k in range(chunk // N_KEYS):
            a = act_s[pl.ds(r0, ACT_ROWS), k * N_KEYS:(k + 1) * N_KEYS]
            sel = jnp.where(ii == i_base + k, jnp.take_along_axis(a, jj, axis=1), sel)
        sel_s[pl.ds(r0, ACT_ROWS), :] = sel
        return carry

    lax.fori_loop(0, tokens // ACT_ROWS, rows, 0, unroll=GATHER_UNROLL)

    @pl.when(c == pl.num_programs(1) - 1)
    def _():
        c_ref[...] = g_ref[...] * jax.nn.gelu(sel_s[...])


def _expert_act(tn, idx_i, idx_j, gate, ut):
    t = tn.shape[0]
    tb = ACT_TOKENS
    tok = lambda w: pl.BlockSpec((tb, w), lambda i, c: (i, 0))
    return pl.pallas_call(
        _expert_act_kernel,
        grid=(t // tb, N_EXPERTS // EXPERT_CHUNK),
        in_specs=[tok(D_MODEL), tok(N_ENTRIES), tok(N_ENTRIES), tok(N_ENTRIES),
                  pl.BlockSpec((D_MODEL, EXPERT_CHUNK), lambda i, c: (0, c))],
        out_specs=tok(N_ENTRIES),
        out_shape=jax.ShapeDtypeStruct((t, N_ENTRIES), F32),
        scratch_shapes=[pltpu.VMEM((tb, EXPERT_CHUNK), F32), pltpu.VMEM((tb, N_ENTRIES), F32)],
        compiler_params=pltpu.CompilerParams(
            dimension_semantics=("arbitrary", "arbitrary"), vmem_limit_bytes=VMEM_LIMIT_BYTES),
        name="expert_act",
    )(tn, idx_i, idx_j, gate, ut)


def _expert_out_kernel(i_ref, j_ref, c_ref, vt_ref, x1_ref, p_ref, gple_ref, wple_ref, wgate_ref,
                       gfin_ref, y_ref, coef_s, acc_s):
    c = pl.program_id(1)
    tokens = x1_ref.shape[0]
    chunk = vt_ref.shape[1]

    @pl.when(c == 0)
    def _():
        acc_s[...] = jnp.zeros_like(acc_s)

        def token(t, carry):
            sub = lax.broadcasted_iota(I32, (N_KEYS, N_ENTRIES), 0)
            pt = jnp.where(sub == i_ref[pl.ds(t, 1), :], c_ref[pl.ds(t, 1), :], 0.0).astype(BF16)
            qt = jnp.where(sub == j_ref[pl.ds(t, 1), :], 1.0, 0.0).astype(BF16)
            coef_s[pl.ds(pl.multiple_of(t * COEF_PITCH, SUBLANES), N_KEYS), :] = _dot_nt(pt, qt)
            return carry

        lax.fori_loop(0, tokens, token, 0, unroll=SCATTER_UNROLL)

    i_base = c * (chunk // N_KEYS)
    lhs = [coef_s[pl.ds(i_base + k, tokens, stride=COEF_PITCH), :].astype(BF16)
           for k in range(chunk // N_KEYS)]
    acc_s[...] += _dot_nt(vt_ref[...], jnp.concatenate(lhs, axis=1))

    @pl.when(c == pl.num_programs(1) - 1)
    def _():
        x2 = x1_ref[...] + acc_s[...].T
        gate = jax.nn.sigmoid(_dot(_rms(x2, gple_ref[...]).astype(BF16), wgate_ref[...]))
        x3 = x2 + _dot(p_ref[...].astype(BF16), wple_ref[...]) * gate
        y_ref[...] = _rms(x3, gfin_ref[...])


def _expert_out(idx_i, idx_j, coef, vtab, x1, p, gple, wple, wgate, gfin):
    t = x1.shape[0]
    tb = OUT_TOKENS
    tok = lambda w: pl.BlockSpec((tb, w), lambda i, c: (i, 0))
    return pl.pallas_call(
        _expert_out_kernel,
        grid=(t // tb, N_EXPERTS // EXPERT_CHUNK),
        in_specs=[tok(N_ENTRIES), tok(N_ENTRIES), tok(N_ENTRIES),
                  pl.BlockSpec((D_MODEL, EXPERT_CHUNK), lambda i, c: (0, c)),
                  tok(D_MODEL), tok(PLE_DIM), _resident(gple.shape), _resident(wple.shape),
                  _resident(wgate.shape), _resident(gfin.shape)],
        out_specs=tok(D_MODEL),
        out_shape=jax.ShapeDtypeStruct((t, D_MODEL), F32),
        scratch_shapes=[pltpu.VMEM((tb * COEF_PITCH, N_KEYS), F32), pltpu.VMEM((D_MODEL, tb), F32)],
        compiler_params=pltpu.CompilerParams(
            dimension_semantics=("arbitrary", "arbitrary"), vmem_limit_bytes=VMEM_LIMIT_BYTES),
        name="expert_out",
    )(idx_i, idx_j, coef, vtab, x1, p, gple, wple, wgate, gfin)


def _s5_params(lam_re, lam_im, log_dt, b_re, b_im, c_re, c_im):
    dt = jnp.exp(log_dt)[:, None]
    mag = jnp.exp(lam_re * dt)
    abr, abi = mag * jnp.cos(lam_im * dt), mag * jnp.sin(lam_im * dt)
    den = lam_re * lam_re + lam_im * lam_im
    nr, ni = abr - 1.0, abi
    coef_r = (nr * lam_re + ni * lam_im) / den
    coef_i = (ni * lam_re - nr * lam_im) / den
    bbr = coef_r[..., None] * b_re - coef_i[..., None] * b_im
    bbi = coef_r[..., None] * b_im + coef_i[..., None] * b_re
    gpb = S5_GROUPS // S5_BLOCKS
    eye = jnp.eye(gpb, dtype=F32)

    def b_blocks(bb):
        bb = bb.reshape(S5_BLOCKS, gpb, S5_STATE, S5_GROUP)
        return jnp.einsum('bgpc,gh->bgchp', bb, eye).reshape(S5_BLOCKS, gpb * S5_GROUP, gpb * S5_STATE)

    def c_blocks(cc):
        cc = cc.reshape(S5_BLOCKS, gpb, S5_GROUP, S5_STATE)
        return jnp.einsum('bgcp,gh->bgphc', cc, eye).reshape(S5_BLOCKS, gpb * S5_STATE, gpb * S5_GROUP)

    bmat = jnp.concatenate([b_blocks(bbr), b_blocks(bbi)], axis=2).astype(BF16)
    cmat = jnp.concatenate([c_blocks(c_re), -c_blocks(c_im)], axis=1).astype(BF16)
    return abr.reshape(1, S5_HID), abi.reshape(1, S5_HID), bmat, cmat


def kernel(x_prompt, x_sample, state_s5_re, state_s5_im, state_lru, state_conv, p_prompt, p_sample, g_mix, w_in, s5_lam_re, s5_lam_im, s5_log_dt, s5_b_re, s5_b_im, s5_c_re, s5_c_im, s5_d, s5_w_glu, s5_b_glu, conv_w, conv_b, lru_w_r, lru_b_r, lru_w_i, lru_b_i, lru_lam, w_a_proj, w_b_proj, w_out, g_ffn, peer_w_q, peer_sub_keys, peer_u, peer_v, g_ple, w_ple, w_ple_gate, g_final):
    depth = g_mix.shape[0]
    assert depth == 1, "one layer per call"
    n_p, l_p, _ = x_prompt.shape
    n_s, l_s, _ = x_sample.shape
    row = lambda v: v.reshape(1, -1)

    a_re, a_im, bmat, cmat = _s5_params(s5_lam_re[0], s5_lam_im[0], s5_log_dt[0], s5_b_re[0], s5_b_im[0],
                                        s5_c_re[0], s5_c_im[0])
    w_ri = jnp.concatenate([lru_w_r[0], lru_w_i[0]], axis=2).astype(BF16)
    mixer_w = (row(g_mix[0]), w_in[0].astype(BF16), a_re, a_im, bmat, cmat, row(s5_d[0]),
               s5_w_glu[0].astype(BF16), row(s5_b_glu[0]), conv_w[0], row(conv_b[0]), w_ri,
               row(lru_b_r[0]), row(lru_b_i[0]), row(lru_lam[0]), w_a_proj[0].astype(BF16),
               w_b_proj[0].astype(BF16), w_out[0].astype(BF16))

    tl_p = MIX_ROWS // n_p
    xp = x_prompt.transpose(1, 0, 2).reshape(l_p * n_p, D_MODEL)
    halo_p = (CONV_WIDTH - 1) * n_p
    x1_p, s5_p, lru_p, conv_p = _mixer(
        xp, jnp.zeros((n_p, 2 * S5_HID), F32), jnp.zeros((n_p, LRU_WIDTH), F32),
        jnp.zeros((halo_p, LRU_WIDTH), F32), mixer_w, groups=1, nb=n_p, tl=tl_p, chunks=l_p // tl_p)

    nb_s = MIX_ROWS // l_s
    grp = n_s // nb_s
    xs = x_sample.reshape(grp, nb_s, l_s, D_MODEL).transpose(0, 2, 1, 3).reshape(n_s * l_s, D_MODEL)
    s5_0 = jnp.concatenate([state_s5_re[0].reshape(n_s, S5_HID), state_s5_im[0].reshape(n_s, S5_HID)], axis=1)
    conv_0 = (state_conv[0].reshape(grp, nb_s, CONV_WIDTH - 1, LRU_WIDTH).transpose(0, 2, 1, 3)
              .reshape(grp * (CONV_WIDTH - 1) * nb_s, LRU_WIDTH))
    x1_s, s5_s, lru_s, conv_s = _mixer(xs, s5_0, state_lru[0], conv_0, mixer_w,
                                       groups=grp, nb=nb_s, tl=l_s, chunks=1)

    x1 = jnp.concatenate([x1_p, x1_s], axis=0)
    pp = jnp.concatenate([
        p_prompt[0].transpose(1, 0, 2).reshape(l_p * n_p, PLE_DIM),
        p_sample[0].reshape(grp, nb_s, l_s, PLE_DIM).transpose(0, 2, 1, 3).reshape(n_s * l_s, PLE_DIM)], axis=0)
    keys = peer_sub_keys[0].reshape(2 * PEER_HEADS, N_KEYS, KEY_HALF).astype(BF16)
    tn, idx_i, idx_j, gate = _topk(x1, row(g_ffn[0]), peer_w_q[0].T.astype(BF16), keys)
    idx_i, idx_j, gate = idx_i.T, idx_j.T, gate.T
    coef = _expert_act(tn, idx_i, idx_j, gate, peer_u[0].T.astype(BF16))
    y = _expert_out(idx_i, idx_j, coef, peer_v[0].T.astype(BF16), x1, pp, row(g_ple[0]),
                    w_ple[0].astype(BF16), w_ple_gate[0].astype(BF16), row(g_final))

    t_p = l_p * n_p
    y_prompt = y[:t_p].reshape(l_p, n_p, D_MODEL).transpose(1, 0, 2)
    y_sample = y[t_p:].reshape(grp, l_s, nb_s, D_MODEL).transpose(0, 2, 1, 3).reshape(n_s, l_s, D_MODEL)

    def states(s5, lru, conv, n, groups, nb):
        re = s5[:, :S5_HID].reshape(1, n, S5_GROUPS, S5_STATE)
        im = s5[:, S5_HID:].reshape(1, n, S5_GROUPS, S5_STATE)
        cv = (conv.reshape(groups, CONV_WIDTH - 1, nb, LRU_WIDTH).transpose(0, 2, 1, 3)
              .reshape(1, n, CONV_WIDTH - 1, LRU_WIDTH))
        return re, im, lru.reshape(1, n, LRU_WIDTH), cv

    sp = states(s5_p, lru_p, conv_p, n_p, 1, n_p)
    ss = states(s5_s, lru_s, conv_s, n_s, grp, nb_s)
    return (y_prompt, y_sample) + sp + ss
```

```python
import functools

import jax
import jax.numpy as jnp
from jax import lax
from jax.experimental import pallas as pl
from jax.experimental.pallas import tpu as pltpu

F32 = jnp.float32
BF16 = jnp.bfloat16
I32 = jnp.int32

D_MODEL = 1024
S5_WIDTH = 512
S5_GROUP = 16
S5_GROUPS = 32
S5_STATE = 64
S5_HID = S5_GROUPS * S5_STATE
S5_BLOCKS = 4
LRU_WIDTH = 1024
LRU_HEADS = 8
LRU_HEAD_DIM = 128
CONV_WIDTH = 4
LRU_C = 8.0
IN_COLS = S5_WIDTH + 2 * LRU_WIDTH + 2 * D_MODEL
PEER_HEADS = 8
N_KEYS = 128
PEER_TOPK = 16
KEY_HALF = 128
N_EXPERTS = N_KEYS * N_KEYS
N_ENTRIES = PEER_HEADS * PEER_TOPK
PLE_DIM = 256
EPS = 1e-6

LANES = 128
SUBLANES = 8
VMEM_LIMIT_BYTES = 60000 * 1024

MIX_ROWS = 256
TOPK_TOKENS = 512
ACT_TOKENS = 1024
OUT_TOKENS = 256
EXPERT_CHUNK = 2048
COEF_PITCH = N_KEYS + SUBLANES
S5_SCAN_ELEMS = 4 * SUBLANES * LANES
LRU_SCAN_ELEMS = 8 * SUBLANES * LANES
SCAN_UNROLL = 4


def _rms(x, g):
    return x * lax.rsqrt(jnp.mean(x * x, axis=-1, keepdims=True) + EPS) * g


def _resident(shape):
    return pl.BlockSpec(shape, lambda *_: (0,) * len(shape), pipeline_mode=pl.Buffered(1))


def _dot(a, b):
    return jnp.dot(a, b, preferred_element_type=F32)


def _dot_nt(a, b):
    return lax.dot_general(a, b, (((1,), (1,)), ((), ())), preferred_element_type=F32)


def _mixer_kernel(nb, tl,
                  x_ref, s5h0_ref, lruh0_ref, conv0_ref, gmix_ref, win_ref, are_ref, aim_ref,
                  bmat_ref, cmat_ref, dskip_ref, wglu_ref, bglu_ref, convw_ref, convb_ref,
                  wri_ref, br_ref, bi_ref, lam_ref, wa_ref, wb_ref, wout_ref,
                  x1_ref, s5out_ref, lruout_ref, convout_ref,
                  proj_s, bu_s, xpad_s, a_s, h_s, s5c_s, lruc_s):
    ci = pl.program_id(1)
    rows = nb * tl
    halo = (CONV_WIDTH - 1) * nb

    @pl.when(ci == 0)
    def _():
        s5c_s[...] = s5h0_ref[...]
        lruc_s[...] = lruh0_ref[...]
        xpad_s[0:halo, :] = conv0_ref[...]

    @pl.when(ci > 0)
    def _():
        xpad_s[0:halo, :] = xpad_s[rows:rows + halo, :]

    x = x_ref[...]
    xn = _rms(x, gmix_ref[...])
    proj_s[...] = _dot(xn.astype(BF16), win_ref[...])

    blk = S5_HID // S5_BLOCKS
    for b in range(S5_BLOCKS):
        ub = proj_s[:, b * LANES:(b + 1) * LANES].astype(BF16)
        bu = _dot(ub, bmat_ref[b])
        bu_s[:, b * blk:(b + 1) * blk] = bu[:, :blk]
        bu_s[:, S5_HID + b * blk:S5_HID + (b + 1) * blk] = bu[:, blk:]

    cb = max(LANES, S5_SCAN_ELEMS // nb)
    for c0 in range(0, S5_HID, cb):
        ar = jnp.broadcast_to(are_ref[:, c0:c0 + cb], (nb, cb))
        ai = jnp.broadcast_to(aim_ref[:, c0:c0 + cb], (nb, cb))

        def s5_step(t, carry, c0=c0, ar=ar, ai=ai):
            hr, hi = carry
            r0 = pl.multiple_of(t * nb, nb)
            bur = bu_s[pl.ds(r0, nb), c0:c0 + cb]
            bui = bu_s[pl.ds(r0, nb), S5_HID + c0:S5_HID + c0 + cb]
            nhr = ar * hr - ai * hi + bur
            nhi = ar * hi + ai * hr + bui
            bu_s[pl.ds(r0, nb), c0:c0 + cb] = nhr
            bu_s[pl.ds(r0, nb), S5_HID + c0:S5_HID + c0 + cb] = nhi
            return nhr, nhi

        hr, hi = lax.fori_loop(0, tl, s5_step,
                               (s5c_s[:, c0:c0 + cb], s5c_s[:, S5_HID + c0:S5_HID + c0 + cb]),
                               unroll=SCAN_UNROLL)
        s5c_s[:, c0:c0 + cb] = hr
        s5c_s[:, S5_HID + c0:S5_HID + c0 + cb] = hi

    ys = []
    for b in range(S5_BLOCKS):
        hre = bu_s[:, b * blk:(b + 1) * blk].astype(BF16)
        him = bu_s[:, S5_HID + b * blk:S5_HID + (b + 1) * blk].astype(BF16)
        ys.append(_dot(hre, cmat_ref[b, 0:blk, :]) + _dot(him, cmat_ref[b, blk:2 * blk, :]))
    y = jnp.concatenate(ys, axis=1) + dskip_ref[...] * proj_s[:, 0:S5_WIDTH]
    y = jax.nn.gelu(y)
    ya = y * jax.nn.sigmoid(_dot(y.astype(BF16), wglu_ref[...]) + bglu_ref[...])

    o_xb = S5_WIDTH
    o_gb = o_xb + LRU_WIDTH
    o_ga = o_gb + LRU_WIDTH
    o_gbm = o_ga + D_MODEL
    xpad_s[halo:halo + rows, :] = proj_s[:, o_xb:o_gb]
    xc = convb_ref[...] + xpad_s[0:rows, :] * convw_ref[0:1, :]
    for k in range(1, CONV_WIDTH):
        xc = xc + xpad_s[k * nb:k * nb + rows, :] * convw_ref[k:k + 1, :]
    xcb = xc.astype(BF16)
    rs, gs = [], []
    for h in range(LRU_HEADS):
        ri = _dot(xcb[:, h * LRU_HEAD_DIM:(h + 1) * LRU_HEAD_DIM], wri_ref[h])
        rs.append(ri[:, :LRU_HEAD_DIM])
        gs.append(ri[:, LRU_HEAD_DIM:])
    r = jax.nn.sigmoid(jnp.concatenate(rs, axis=1) + br_ref[...])
    ig = jax.nn.sigmoid(jnp.concatenate(gs, axis=1) + bi_ref[...])
    lam = lam_ref[...]
    softplus_neg = jnp.maximum(-lam, 0.0) + jnp.log1p(jnp.exp(-jnp.abs(lam)))
    log_a = -LRU_C * r * softplus_neg
    a = jnp.exp(log_a)
    mult = jnp.sqrt(-jnp.tanh(log_a) * (a * a + 1.0))
    a_s[...] = a
    h_s[...] = mult * ig * xc

    lb = max(LANES, LRU_SCAN_ELEMS // nb)
    for c0 in range(0, LRU_WIDTH, lb):
        def lru_step(t, h, c0=c0):
            r0 = pl.multiple_of(t * nb, nb)
            nh = a_s[pl.ds(r0, nb), c0:c0 + lb] * h + h_s[pl.ds(r0, nb), c0:c0 + lb]
            h_s[pl.ds(r0, nb), c0:c0 + lb] = nh
            return nh

        lruc_s[:, c0:c0 + lb] = lax.fori_loop(0, tl, lru_step, lruc_s[:, c0:c0 + lb], unroll=SCAN_UNROLL)
    yb = jax.nn.gelu(proj_s[:, o_gb:o_ga]) * h_s[...]

    ya_d = _dot(ya.astype(BF16), wa_ref[...])
    yb_d = _dot(yb.astype(BF16), wb_ref[...])
    merged = (jax.nn.sigmoid(proj_s[:, o_ga:o_gbm]) * ya_d
              + jax.nn.sigmoid(proj_s[:, o_gbm:IN_COLS]) * yb_d)
    x1_ref[...] = x + _dot(merged.astype(BF16), wout_ref[...])

    @pl.when(ci == pl.num_programs(1) - 1)
    def _():
        s5out_ref[...] = s5c_s[...]
        lruout_ref[...] = lruc_s[...]
        convout_ref[...] = xpad_s[rows:rows + halo, :]


def _mixer(x_tm, s5h0, lruh0, conv0, weights, *, groups, nb, tl, chunks):
    rows = nb * tl
    halo = (CONV_WIDTH - 1) * nb
    w_specs = [_resident(w.shape) for w in weights]
    in_specs = [
        pl.BlockSpec((rows, D_MODEL), lambda g, c: (g * chunks + c, 0)),
        pl.BlockSpec((nb, 2 * S5_HID), lambda g, c: (g, 0)),
        pl.BlockSpec((nb, LRU_WIDTH), lambda g, c: (g, 0)),
        pl.BlockSpec((halo, LRU_WIDTH), lambda g, c: (g, 0)),
    ] + w_specs
    out_specs = [
        pl.BlockSpec((rows, D_MODEL), lambda g, c: (g * chunks + c, 0)),
        pl.BlockSpec((nb, 2 * S5_HID), lambda g, c: (g, 0)),
        pl.BlockSpec((nb, LRU_WIDTH), lambda g, c: (g, 0)),
        pl.BlockSpec((halo, LRU_WIDTH), lambda g, c: (g, 0)),
    ]
    out_shape = [
        jax.ShapeDtypeStruct(x_tm.shape, F32),
        jax.ShapeDtypeStruct((groups * nb, 2 * S5_HID), F32),
        jax.ShapeDtypeStruct((groups * nb, LRU_WIDTH), F32),
        jax.ShapeDtypeStruct((groups * halo, LRU_WIDTH), F32),
    ]
    scratch = [
        pltpu.VMEM((rows, IN_COLS), F32),
        pltpu.VMEM((rows, 2 * S5_HID), F32),
        pltpu.VMEM((rows + halo, LRU_WIDTH), F32),
        pltpu.VMEM((rows, LRU_WIDTH), F32),
        pltpu.VMEM((rows, LRU_WIDTH), F32),
        pltpu.VMEM((nb, 2 * S5_HID), F32),
        pltpu.VMEM((nb, LRU_WIDTH), F32),
    ]
    return pl.pallas_call(
        functools.partial(_mixer_kernel, nb, tl),
        grid=(groups, chunks),
        in_specs=in_specs, out_specs=out_specs, out_shape=out_shape, scratch_shapes=scratch,
        compiler_params=pltpu.CompilerParams(
            dimension_semantics=("arbitrary", "arbitrary"), vmem_limit_bytes=VMEM_LIMIT_BYTES),
        name="mixer",
    )(x_tm, s5h0, lruh0, conv0, *weights)


def _extract_top(vals, ids, n_out):
    w = vals.shape[1]
    krow = lax.broadcasted_iota(I32, (n_out, w), 0)
    out_v = jnp.zeros((n_out, w), F32)
    out_i = jnp.zeros((n_out, w), I32)
    big = jnp.int32(2 ** 30)
    for k in range(n_out):
        m = jnp.max(vals, axis=0, keepdims=True)
        sel = jnp.min(jnp.where(vals == m, ids, big), axis=0, keepdims=True)
        vals = jnp.where(ids == sel, -jnp.inf, vals)
        out_v = jnp.where(krow == k, m, out_v)
        out_i = jnp.where(krow == k, sel, out_i)
    return out_v, out_i


def _pick_rows(table, sel):
    out = jnp.zeros(sel.shape, table.dtype)
    for r in range(table.shape[0]):
        out = jnp.where(sel == r, table[r:r + 1, :], out)
    return out


def _topk_kernel(x1_ref, gffn_ref, wqt_ref, keys_ref, tn_ref, i_ref, j_ref, g_ref,
                 qt_s, sv_s, si_s):
    tokens = x1_ref.shape[0]
    tn = _rms(x1_ref[...], gffn_ref[...]).astype(BF16)
    tn_ref[...] = tn
    qt_s[...] = _dot_nt(wqt_ref[...], tn).astype(BF16)

    def stage1(hs, carry):
        q = qt_s[pl.ds(pl.multiple_of(hs * KEY_HALF, KEY_HALF), KEY_HALF), :]
        s = _dot(keys_ref[hs], q)
        for c0 in range(0, tokens, LANES):
            ids = lax.broadcasted_iota(I32, (N_KEYS, LANES), 0)
            v, i = _extract_top(s[:, c0:c0 + LANES], ids, PEER_TOPK)
            sv_s[hs, :, c0:c0 + LANES] = v
            si_s[hs, :, c0:c0 + LANES] = i
        return carry

    lax.fori_loop(0, 2 * PEER_HEADS, stage1, 0)

    def stage2(h, carry):
        for c0 in range(0, tokens, LANES):
            sv0 = sv_s[2 * h, :, c0:c0 + LANES]
            sv1 = sv_s[2 * h + 1, :, c0:c0 + LANES]
            row = lax.broadcasted_iota(I32, (SUBLANES, LANES), 0)
            cand, pos = [], []
            for k1 in range(SUBLANES):
                n0 = PEER_TOPK // (k1 + 1)
                for base in range(0, n0, SUBLANES):
                    v = sv0[base:base + SUBLANES, :] + sv1[k1:k1 + 1, :]
                    cand.append(jnp.where(row + base < n0, v, -jnp.inf))
                    pos.append((row + base) * PEER_TOPK + k1)
            cand.append(sv0[0:1, :] + sv1[SUBLANES:2 * SUBLANES, :])
            pos.append(row + SUBLANES)
            top_s, top_p = _extract_top(jnp.concatenate(cand, axis=0), jnp.concatenate(pos, axis=0),
                                        PEER_TOPK)
            k0 = lax.shift_right_logical(top_p, 4)
            k1 = jnp.bitwise_and(top_p, PEER_TOPK - 1)
            e = jnp.exp(top_s - top_s[0:1, :])
            rows = pl.ds(pl.multiple_of(h * PEER_TOPK, PEER_TOPK), PEER_TOPK)
            i_ref[rows, c0:c0 + LANES] = _pick_rows(si_s[2 * h, :, c0:c0 + LANES], k0)
            j_ref[rows, c0:c0 + LANES] = _pick_rows(si_s[2 * h + 1, :, c0:c0 + LANES], k1)
            g_ref[rows, c0:c0 + LANES] = e / jnp.sum(e, axis=0, keepdims=True)
        return carry

    lax.fori_loop(0, PEER_HEADS, stage2, 0)


def _topk(x1, gffn, wqt, keys):
    t = x1.shape[0]
    tb = TOPK_TOKENS
    return pl.pallas_call(
        _topk_kernel,
        grid=(t // tb,),
        in_specs=[pl.BlockSpec((tb, D_MODEL), lambda i: (i, 0)), _resident(gffn.shape),
                  _resident(wqt.shape), _resident(keys.shape)],
        out_specs=[pl.BlockSpec((tb, D_MODEL), lambda i: (i, 0))]
        + [pl.BlockSpec((N_ENTRIES, tb), lambda i: (0, i))] * 3,
        out_shape=[jax.ShapeDtypeStruct((t, D_MODEL), BF16),
                   jax.ShapeDtypeStruct((N_ENTRIES, t), I32),
                   jax.ShapeDtypeStruct((N_ENTRIES, t), I32),
                   jax.ShapeDtypeStruct((N_ENTRIES, t), F32)],
        scratch_shapes=[pltpu.VMEM((2 * PEER_HEADS * KEY_HALF, tb), BF16),
                        pltpu.VMEM((2 * PEER_HEADS, PEER_TOPK, tb), F32),
                        pltpu.VMEM((2 * PEER_HEADS, PEER_TOPK, tb), I32)],
        compiler_params=pltpu.CompilerParams(
            dimension_semantics=("arbitrary",), vmem_limit_bytes=VMEM_LIMIT_BYTES),
        name="topk",
    )(x1, gffn, wqt, keys)


def _gather_selected(act_ref, i_ref, j_ref, sel_s, i_base):
    for r0 in range(0, i_ref.shape[0], SUBLANES):
        ii = i_ref[r0:r0 + SUBLANES, :]
        jj = j_ref[r0:r0 + SUBLANES, :]
        sel = sel_s[r0:r0 + SUBLANES, :]
        for k in range(act_ref.shape[1] // N_KEYS):
            a = act_ref[r0:r0 + SUBLANES, k * N_KEYS:(k + 1) * N_KEYS]
            sel = jnp.where(ii == i_base + k, jnp.take_along_axis(a, jj, axis=1), sel)
        sel_s[r0:r0 + SUBLANES, :] = sel


def _expert_act_kernel(tn_ref, i_ref, j_ref, g_ref, ut_ref, c_ref, act_a, act_b, sel_s):
    b = pl.program_id(0)
    c = pl.program_id(1)
    chunks = pl.num_programs(1) - 1
    slabs = ut_ref.shape[1] // N_KEYS
    i_prev = (c - 1) * slabs

    @pl.when((b == 0) & (c == 0))
    def _():
        act_b[...] = jnp.zeros_like(act_b)

    @pl.when(c == 0)
    def _():
        sel_s[...] = jnp.zeros_like(sel_s)

    def step(act_new, act_old):
        act_new[...] = _dot(tn_ref[...], ut_ref[...])
        _gather_selected(act_old, i_ref, j_ref, sel_s, i_prev)

    even = lax.rem(c, 2) == 0

    @pl.when(even & (c < chunks))
    def _():
        step(act_a, act_b)

    @pl.when(jnp.logical_not(even))
    def _():
        step(act_b, act_a)

    @pl.when(c == chunks)
    def _():
        _gather_selected(act_b if chunks % 2 == 0 else act_a, i_ref, j_ref, sel_s, i_prev)
        c_ref[...] = g_ref[...] * jax.nn.gelu(sel_s[...])


def _expert_act(tn, idx_i, idx_j, gate, ut):
    t = tn.shape[0]
    tb = ACT_TOKENS
    chunks = N_EXPERTS // EXPERT_CHUNK
    tok = lambda w: pl.BlockSpec((tb, w), lambda i, c: (i, 0))
    return pl.pallas_call(
        _expert_act_kernel,
        grid=(t // tb, chunks + 1),
        in_specs=[tok(D_MODEL), tok(N_ENTRIES), tok(N_ENTRIES), tok(N_ENTRIES),
                  pl.BlockSpec((D_MODEL, EXPERT_CHUNK), lambda i, c: (0, jnp.minimum(c, chunks - 1)))],
        out_specs=tok(N_ENTRIES),
        out_shape=jax.ShapeDtypeStruct((t, N_ENTRIES), F32),
        scratch_shapes=[pltpu.VMEM((tb, EXPERT_CHUNK), F32), pltpu.VMEM((tb, EXPERT_CHUNK), F32),
                        pltpu.VMEM((tb, N_ENTRIES), F32)],
        compiler_params=pltpu.CompilerParams(
            dimension_semantics=("arbitrary", "arbitrary"), vmem_limit_bytes=VMEM_LIMIT_BYTES),
        name="expert_act",
    )(tn, idx_i, idx_j, gate, ut)


def _expert_out_kernel(i_ref, j_ref, c_ref, vt_ref, x1_ref, p_ref, gple_ref, wple_ref, wgate_ref,
                       gfin_ref, y_ref, coef_a, coef_b, acc_s):
    b = pl.program_id(0)
    c = pl.program_id(1)
    chunks = pl.num_programs(1)
    tokens = x1_ref.shape[0]
    slabs = vt_ref.shape[1] // N_KEYS
    per_step = tokens // (N_EXPERTS // vt_ref.shape[1])

    @pl.when((b == 0) & (c == 0))
    def _():
        coef_b[...] = jnp.zeros_like(coef_b)

    @pl.when(c == 0)
    def _():
        acc_s[...] = jnp.zeros_like(acc_s)

    def step(coef_new, coef_old):
        for k in range(per_step):
            t = c * per_step + k
            sub = lax.broadcasted_iota(I32, (N_KEYS, N_ENTRIES), 0)
            pt = jnp.where(sub == i_ref[pl.ds(t, 1), :], c_ref[pl.ds(t, 1), :], 0.0).astype(BF16)
            qt = jnp.where(sub == j_ref[pl.ds(t, 1), :], 1.0, 0.0).astype(BF16)
            coef_new[pl.ds(pl.multiple_of(t * COEF_PITCH, SUBLANES), N_KEYS), :] = _dot_nt(pt, qt)
        lhs = [coef_old[pl.ds(c * slabs + k, tokens, stride=COEF_PITCH), :].astype(BF16)
               for k in range(slabs)]
        acc_s[...] += _dot_nt(vt_ref[...], jnp.concatenate(lhs, axis=1))

    even = lax.rem(b, 2) == 0

    @pl.when(even)
    def _():
        step(coef_a, coef_b)

    @pl.when(jnp.logical_not(even))
    def _():
        step(coef_b, coef_a)

    @pl.when((c == chunks - 1) & (b > 0))
    def _():
        x2 = x1_ref[...] + acc_s[...].T
        gate = jax.nn.sigmoid(_dot(_rms(x2, gple_ref[...]).astype(BF16), wgate_ref[...]))
        x3 = x2 + _dot(p_ref[...].astype(BF16), wple_ref[...]) * gate
        y_ref[...] = _rms(x3, gfin_ref[...])


def _expert_out(idx_i, idx_j, coef, vtab, x1, p, gple, wple, wgate, gfin):
    t = x1.shape[0]
    tb = OUT_TOKENS
    nblk = t // tb
    scat = lambda w: pl.BlockSpec((tb, w), lambda i, c: (jnp.minimum(i, nblk - 1), 0))
    tok = lambda w: pl.BlockSpec((tb, w), lambda i, c: (jnp.maximum(i - 1, 0), 0))
    return pl.pallas_call(
        _expert_out_kernel,
        grid=(nblk + 1, N_EXPERTS // EXPERT_CHUNK),
        in_specs=[scat(N_ENTRIES), scat(N_ENTRIES), scat(N_ENTRIES),
                  pl.BlockSpec((D_MODEL, EXPERT_CHUNK), lambda i, c: (0, c)),
                  tok(D_MODEL), tok(PLE_DIM), _resident(gple.shape), _resident(wple.shape),
                  _resident(wgate.shape), _resident(gfin.shape)],
        out_specs=tok(D_MODEL),
        out_shape=jax.ShapeDtypeStruct((t, D_MODEL), F32),
        scratch_shapes=[pltpu.VMEM((tb * COEF_PITCH, N_KEYS), F32),
                        pltpu.VMEM((tb * COEF_PITCH, N_KEYS), F32), pltpu.VMEM((D_MODEL, tb), F32)],
        compiler_params=pltpu.CompilerParams(
            dimension_semantics=("arbitrary", "arbitrary"), vmem_limit_bytes=VMEM_LIMIT_BYTES),
        name="expert_out",
    )(idx_i, idx_j, coef, vtab, x1, p, gple, wple, wgate, gfin)


def _s5_params(lam_re, lam_im, log_dt, b_re, b_im, c_re, c_im):
    dt = jnp.exp(log_dt)[:, None]
    mag = jnp.exp(lam_re * dt)
    abr, abi = mag * jnp.cos(lam_im * dt), mag * jnp.sin(lam_im * dt)
    den = lam_re * lam_re + lam_im * lam_im
    nr, ni = abr - 1.0, abi
    coef_r = (nr * lam_re + ni * lam_im) / den
    coef_i = (ni * lam_re - nr * lam_im) / den
    bbr = coef_r[..., None] * b_re - coef_i[..., None] * b_im
    bbi = coef_r[..., None] * b_im + coef_i[..., None] * b_re
    gpb = S5_GROUPS // S5_BLOCKS
    eye = jnp.eye(gpb, dtype=F32)

    def b_blocks(bb):
        bb = bb.reshape(S5_BLOCKS, gpb, S5_STATE, S5_GROUP)
        return jnp.einsum('bgpc,gh->bgchp', bb, eye).reshape(S5_BLOCKS, gpb * S5_GROUP, gpb * S5_STATE)

    def c_blocks(cc):
        cc = cc.reshape(S5_BLOCKS, gpb, S5_GROUP, S5_STATE)
        return jnp.einsum('bgcp,gh->bgphc', cc, eye).reshape(S5_BLOCKS, gpb * S5_STATE, gpb * S5_GROUP)

    bmat = jnp.concatenate([b_blocks(bbr), b_blocks(bbi)], axis=2).astype(BF16)
    cmat = jnp.concatenate([c_blocks(c_re), -c_blocks(c_im)], axis=1).astype(BF16)
    return abr.reshape(1, S5_HID), abi.reshape(1, S5_HID), bmat, cmat


def kernel(x_prompt, x_sample, state_s5_re, state_s5_im, state_lru, state_conv, p_prompt, p_sample, g_mix, w_in, s5_lam_re, s5_lam_im, s5_log_dt, s5_b_re, s5_b_im, s5_c_re, s5_c_im, s5_d, s5_w_glu, s5_b_glu, conv_w, conv_b, lru_w_r, lru_b_r, lru_w_i, lru_b_i, lru_lam, w_a_proj, w_b_proj, w_out, g_ffn, peer_w_q, peer_sub_keys, peer_u, peer_v, g_ple, w_ple, w_ple_gate, g_final):
    depth = g_mix.shape[0]
    assert depth == 1, "one layer per call"
    n_p, l_p, _ = x_prompt.shape
    n_s, l_s, _ = x_sample.shape
    row = lambda v: v.reshape(1, -1)

    a_re, a_im, bmat, cmat = _s5_params(s5_lam_re[0], s5_lam_im[0], s5_log_dt[0], s5_b_re[0], s5_b_im[0],
                                        s5_c_re[0], s5_c_im[0])
    w_ri = jnp.concatenate([lru_w_r[0], lru_w_i[0]], axis=2).astype(BF16)
    mixer_w = (row(g_mix[0]), w_in[0].astype(BF16), a_re, a_im, bmat, cmat, row(s5_d[0]),
               s5_w_glu[0].astype(BF16), row(s5_b_glu[0]), conv_w[0], row(conv_b[0]), w_ri,
               row(lru_b_r[0]), row(lru_b_i[0]), row(lru_lam[0]), w_a_proj[0].astype(BF16),
               w_b_proj[0].astype(BF16), w_out[0].astype(BF16))

    tl_p = MIX_ROWS // n_p
    xp = x_prompt.transpose(1, 0, 2).reshape(l_p * n_p, D_MODEL)
    halo_p = (CONV_WIDTH - 1) * n_p
    x1_p, s5_p, lru_p, conv_p = _mixer(
        xp, jnp.zeros((n_p, 2 * S5_HID), F32), jnp.zeros((n_p, LRU_WIDTH), F32),
        jnp.zeros((halo_p, LRU_WIDTH), F32), mixer_w, groups=1, nb=n_p, tl=tl_p, chunks=l_p // tl_p)

    nb_s = MIX_ROWS // l_s
    grp = n_s // nb_s
    xs = x_sample.reshape(grp, nb_s, l_s, D_MODEL).transpose(0, 2, 1, 3).reshape(n_s * l_s, D_MODEL)
    s5_0 = jnp.concatenate([state_s5_re[0].reshape(n_s, S5_HID), state_s5_im[0].reshape(n_s, S5_HID)], axis=1)
    conv_0 = (state_conv[0].reshape(grp, nb_s, CONV_WIDTH - 1, LRU_WIDTH).transpose(0, 2, 1, 3)
              .reshape(grp * (CONV_WIDTH - 1) * nb_s, LRU_WIDTH))
    x1_s, s5_s, lru_s, conv_s = _mixer(xs, s5_0, state_lru[0], conv_0, mixer_w,
                                       groups=grp, nb=nb_s, tl=l_s, chunks=1)

    x1 = jnp.concatenate([x1_p, x1_s], axis=0)
    pp = jnp.concatenate([
        p_prompt[0].transpose(1, 0, 2).reshape(l_p * n_p, PLE_DIM),
        p_sample[0].reshape(grp, nb_s, l_s, PLE_DIM).transpose(0, 2, 1, 3).reshape(n_s * l_s, PLE_DIM)], axis=0)
    keys = peer_sub_keys[0].reshape(2 * PEER_HEADS, N_KEYS, KEY_HALF).astype(BF16)
    tn, idx_i, idx_j, gate = _topk(x1, row(g_ffn[0]), peer_w_q[0].T.astype(BF16), keys)
    idx_i, idx_j, gate = idx_i.T, idx_j.T, gate.T
    coef = _expert_act(tn, idx_i, idx_j, gate, peer_u[0].T.astype(BF16))
    y = _expert_out(idx_i, idx_j, coef, peer_v[0].T.astype(BF16), x1, pp, row(g_ple[0]),
                    w_ple[0].astype(BF16), w_ple_gate[0].astype(BF16), row(g_final))

    t_p = l_p * n_p
    y_prompt = y[:t_p].reshape(l_p, n_p, D_MODEL).transpose(1, 0, 2)
    y_sample = y[t_p:].reshape(grp, l_s, nb_s, D_MODEL).transpose(0, 2, 1, 3).reshape(n_s, l_s, D_MODEL)

    def states(s5, lru, conv, n, groups, nb):
        re = s5[:, :S5_HID].reshape(1, n, S5_GROUPS, S5_STATE)
        im = s5[:, S5_HID:].reshape(1, n, S5_GROUPS, S5_STATE)
        cv = (conv.reshape(groups, CONV_WIDTH - 1, nb, LRU_WIDTH).transpose(0, 2, 1, 3)
              .reshape(1, n, CONV_WIDTH - 1, LRU_WIDTH))
        return re, im, lru.reshape(1, n, LRU_WIDTH), cv

    sp = states(s5_p, lru_p, conv_p, n_p, 1, n_p)
    ss = states(s5_s, lru_s, conv_s, n_s, grp, nb_s)
    return (y_prompt, y_sample) + sp + ss
```
